```python
import jax, jax.numpy as jnp
from jax import lax
import numpy as np

D_MODEL = 2048
BATCH = 4
SEQ = 4096
DEPTH = 2

CHUNK = 64
EPS = 1e-6
N_MOD = 6
LRU_WIDTH = 1024
LRU_HEADS = 16
LRU_HEAD_DIM = LRU_WIDTH // LRU_HEADS
LRU_CONV = 4
LRU_C = 8.0
SC_WIDTH = 1024
SC_K = 3
POOL_WIDTH = 1024
POOL_WINDOWS = (2, 4, 8, 16)
POOL_GROUPS = 4
POOL_GROUP_DIM = POOL_WIDTH // POOL_GROUPS
POOL_OUT_DIM = D_MODEL // POOL_GROUPS
N_BRANCH = 3
IN_WIDTHS = (LRU_WIDTH, LRU_WIDTH, SC_WIDTH, SC_WIDTH, SC_WIDTH, POOL_WIDTH, N_BRANCH * D_MODEL)
IN_WIDTH = 2 * LRU_WIDTH + 3 * SC_WIDTH + POOL_WIDTH + N_BRANCH * D_MODEL
D_FF = -(-8 * D_MODEL // (3 * 256)) * 256

kernel_name = "hybrid_rglru_shortconv_pool_adaln_block"


def rms_norm(x, gain):
    x32 = x.astype(jnp.float32)
    y = x32 * lax.rsqrt(jnp.mean(x32 * x32, axis=-1, keepdims=True) + EPS)
    return (y * gain.astype(jnp.float32)).astype(x.dtype)


def causal_depthwise_conv(u, w):
    k, ch = w.shape
    return lax.conv_general_dilated(
        u, w[:, None, :].astype(u.dtype), window_strides=(1,), padding=((k - 1, 0),),
        dimension_numbers=("NWC", "WIO", "NWC"), feature_group_count=ch)


def _linear_recurrence_combine(left, right):
    a1, b1 = left
    a2, b2 = right
    return a1 * a2, a2 * b1 + b2


def rglru_branch(u_x, u_gate, conv_w, conv_b, w_a, b_a, w_x, b_x, lam, w_out):
    bsz, s, _ = u_x.shape
    xc = (causal_depthwise_conv(u_x, conv_w) + conv_b).astype(jnp.float32)
    xh = xc.reshape(bsz, s, LRU_HEADS, LRU_HEAD_DIM)
    r = jax.nn.sigmoid(jnp.einsum("bshi,hij->bshj", xh, w_a.astype(jnp.float32)).reshape(bsz, s, LRU_WIDTH) + b_a)
    i = jax.nn.sigmoid(jnp.einsum("bshi,hij->bshj", xh, w_x.astype(jnp.float32)).reshape(bsz, s, LRU_WIDTH) + b_x)
    log_a = -LRU_C * r * jax.nn.softplus(-lam.astype(jnp.float32))
    a = jnp.exp(log_a)
    mult = jnp.sqrt(-jnp.expm1(2.0 * log_a))
    mult = jnp.where((jnp.arange(s) == 0)[None, :, None], 1.0, mult)
    b = mult * (i * xc)
    _, h = lax.associative_scan(_linear_recurrence_combine, (a, b), axis=1)
    y = h * jax.nn.gelu(u_gate.astype(jnp.float32))
    return y.astype(u_x.dtype) @ w_out


def shortconv_branch(u_b, u_c, u_x, conv_w, w_out):
    v = causal_depthwise_conv(u_c * u_x, conv_w)
    return (u_b * v) @ w_out


def pool_branch(u, w_group, scale):
    bsz, s, _ = u.shape
    u32 = u.astype(jnp.float32)
    cs = jnp.pad(jnp.cumsum(u32, axis=1), ((0, 0), (1, 0), (0, 0)))
    t = jnp.arange(s)
    outs = []
    for g, w in enumerate(POOL_WINDOWS):
        sl = slice(g * POOL_GROUP_DIM, (g + 1) * POOL_GROUP_DIM)
        hi = cs[:, 1:, sl]
        lo = jnp.pad(cs[:, : s + 1 - w, sl], ((0, 0), (w - 1, 0), (0, 0)))
        cnt = jnp.minimum(t + 1, w).astype(jnp.float32)[None, :, None]
        outs.append((hi - lo) / cnt - u32[:, :, sl])
    pooled = jnp.stack(outs, axis=2).astype(u.dtype)
    y = jnp.einsum("bsgi,gio->bsgo", pooled, w_group).reshape(bsz, s, D_MODEL)
    return y * scale


def setup_inputs(seed: int = 0) -> dict:
    key = jax.random.key(seed)
    ks = jax.random.split(key, 32)

    def nrm(k, shape, scale):
        return jax.random.normal(k, shape, jnp.float32) * scale

    u = jax.random.uniform(ks[13], (DEPTH, LRU_WIDTH), jnp.float32, 0.9, 0.999)
    sig = u ** (1.0 / LRU_C)
    lru_lambda = jnp.log(sig) - jnp.log1p(-sig)
    return {
        "x": nrm(ks[0], (BATCH, SEQ, D_MODEL), 1.0),
        "c": nrm(ks[1], (BATCH, D_MODEL), 1.0),
        "w_mod": nrm(ks[2], (DEPTH, D_MODEL, N_MOD * D_MODEL), 0.5 * D_MODEL ** -0.5),
        "b_mod": nrm(ks[3], (DEPTH, N_MOD * D_MODEL), 0.02),
        "norm_mix": 1.0 + nrm(ks[4], (DEPTH, D_MODEL), 0.05),
        "w_in": nrm(ks[5], (DEPTH, D_MODEL, IN_WIDTH), D_MODEL ** -0.5),
        "lru_conv_w": nrm(ks[6], (DEPTH, LRU_CONV, LRU_WIDTH), LRU_CONV ** -0.5),
        "lru_conv_b": nrm(ks[7], (DEPTH, LRU_WIDTH), 0.01),
        "lru_w_a": nrm(ks[8], (DEPTH, LRU_HEADS, LRU_HEAD_DIM, LRU_HEAD_DIM), LRU_HEAD_DIM ** -0.5),
        "lru_b_a": nrm(ks[9], (DEPTH, LRU_WIDTH), 0.01),
        "lru_w_x": nrm(ks[10], (DEPTH, LRU_HEADS, LRU_HEAD_DIM, LRU_HEAD_DIM), LRU_HEAD_DIM ** -0.5),
        "lru_b_x": nrm(ks[11], (DEPTH, LRU_WIDTH), 0.01),
        "lru_lambda": lru_lambda,
        "lru_w_out": nrm(ks[12], (DEPTH, LRU_WIDTH, D_MODEL), LRU_WIDTH ** -0.5),
        "sc_conv_w": nrm(ks[14], (DEPTH, SC_K, SC_WIDTH), SC_K ** -0.5),
        "sc_w_out": nrm(ks[15], (DEPTH, SC_WIDTH, D_MODEL), SC_WIDTH ** -0.5),
        "pool_w": nrm(ks[16], (DEPTH, POOL_GROUPS, POOL_GROUP_DIM, POOL_OUT_DIM), POOL_GROUP_DIM ** -0.5),
        "pool_scale": 1.0 + nrm(ks[17], (DEPTH, D_MODEL), 0.1),
        "w_o": nrm(ks[18], (DEPTH, D_MODEL, D_MODEL), D_MODEL ** -0.5),
        "norm_ffn": 1.0 + nrm(ks[19], (DEPTH, D_MODEL), 0.05),
        "ffn_w1": nrm(ks[20], (DEPTH, D_MODEL, D_FF), D_MODEL ** -0.5),
        "ffn_w3": nrm(ks[21], (DEPTH, D_MODEL, D_FF), D_MODEL ** -0.5),
        "ffn_w2": nrm(ks[22], (DEPTH, D_FF, D_MODEL), D_FF ** -0.5),
        "norm_final": 1.0 + nrm(ks[23], (D_MODEL,), 0.05),
    }


def reference(x, c, w_mod, b_mod, norm_mix, w_in, lru_conv_w, lru_conv_b, lru_w_a, lru_b_a, lru_w_x, lru_b_x,
              lru_lambda, lru_w_out, sc_conv_w, sc_w_out, pool_w, pool_scale, w_o, norm_ffn, ffn_w1, ffn_w3,
              ffn_w2, norm_final):
    bsz, s, _ = x.shape
    split_at = np.cumsum(IN_WIDTHS)[:-1].tolist()
    cond = jax.nn.silu(c)
    for l in range(DEPTH):
        mod = (cond @ w_mod[l] + b_mod[l])[:, None, :]
        sh1, sc1, g1, sh2, sc2, g2 = jnp.split(mod, N_MOD, axis=-1)

        h = rms_norm(x, norm_mix[l]) * (1.0 + sc1) + sh1
        p = h @ w_in[l]
        lru_x, lru_g, sc_b, sc_c, sc_x, pool_u, gate_logits = jnp.split(p, split_at, axis=-1)
        y_lru = rglru_branch(lru_x, lru_g, lru_conv_w[l], lru_conv_b[l], lru_w_a[l], lru_b_a[l],
                             lru_w_x[l], lru_b_x[l], lru_lambda[l], lru_w_out[l])
        y_sc = shortconv_branch(sc_b, sc_c, sc_x, sc_conv_w[l], sc_w_out[l])
        y_pool = pool_branch(pool_u, pool_w[l], pool_scale[l])
        gates = jax.nn.sigmoid(gate_logits).reshape(bsz, s, N_BRANCH, D_MODEL)
        merged = gates[:, :, 0] * y_lru + gates[:, :, 1] * y_sc + gates[:, :, 2] * y_pool
        x = x + g1 * (merged @ w_o[l])

        h2 = rms_norm(x, norm_ffn[l]) * (1.0 + sc2) + sh2
        ff = (jax.nn.silu(h2 @ ffn_w1[l]) * (h2 @ ffn_w3[l])) @ ffn_w2[l]
        x = x + g2 * ff
    return rms_norm(x, norm_final)
```

```python
import functools

import jax
import jax.numpy as jnp
from jax import lax
from jax.experimental import pallas as pl
from jax.experimental.pallas import tpu as pltpu

F32 = jnp.float32
BF16 = jnp.bfloat16

EPS = 1e-6
LRU_C = 8.0
LRU_HEAD_DIM = 64
POOL_WINDOWS = (2, 4, 8, 16)
N_MOD = 6

SUBLANES = 8
CHUNK = 256
ROW_TILE = 512
MERGE_ROW_TILE = 1024
MERGE_COL_TILE = 512
OUT_ROW_TILE = 512
FFN_ROW_TILE = 512
FFN_COL_TILE = 512
MOD_COL_TILE = 1024
VMEM_LIMIT = 56 * 1024 * 1024


def _params(sem):
    return pltpu.CompilerParams(dimension_semantics=sem, vmem_limit_bytes=VMEM_LIMIT)


def _sigmoid(x):
    return 0.5 * jnp.tanh(0.5 * x) + 0.5


def _gelu_tanh(x):
    return 0.5 * x * (1.0 + jnp.tanh(0.7978845608028654 * (x + 0.044715 * (x * x * x))))


def _mod_norm(x, gain, scale, shift):
    y = x * lax.rsqrt(jnp.mean(x * x, axis=-1, keepdims=True) + EPS)
    return (y * gain) * (1.0 + scale) + shift


def _shift_rows(ext, s):
    if s == 0:
        return ext
    return pltpu.roll(ext, s, 0)


def _mod_kernel(c_ref, w_ref, b_ref, o_ref):
    c = c_ref[...]
    s = (c * _sigmoid(c)).astype(BF16)
    o_ref[0] = jnp.dot(s, w_ref[0].astype(BF16), preferred_element_type=F32) + b_ref[0]


def _modulation(c, w_mod, b_mod):
    depth, d, width = w_mod.shape
    bsz = c.shape[0]
    c_pad = jnp.zeros((SUBLANES, d), F32).at[:bsz].set(c)
    out = pl.pallas_call(
        _mod_kernel,
        grid=(depth, width // MOD_COL_TILE),
        in_specs=[
            pl.BlockSpec((SUBLANES, d), lambda l, n: (0, 0)),
            pl.BlockSpec((1, d, MOD_COL_TILE), lambda l, n: (l, 0, n)),
            pl.BlockSpec((1, 1, MOD_COL_TILE), lambda l, n: (l, 0, n)),
        ],
        out_specs=pl.BlockSpec((1, SUBLANES, MOD_COL_TILE), lambda l, n: (l, 0, n)),
        out_shape=jax.ShapeDtypeStruct((depth, SUBLANES, width), F32),
        compiler_params=_params(("arbitrary", "arbitrary")),
        name="modulation",
    )(c_pad, w_mod, b_mod.reshape(depth, 1, width))
    return out[:, :bsz].reshape(depth, bsz, N_MOD, 1, d).transpose(0, 2, 1, 3, 4)


def _norm_kernel(x_ref, gain_ref, sc_ref, sh_ref, o_ref):
    o_ref[0] = _mod_norm(x_ref[0], gain_ref[...], sc_ref[0], sh_ref[0]).astype(BF16)


def _first_norm(x, gain, sc, sh):
    bsz, s, d = x.shape
    t = ROW_TILE
    vec = pl.BlockSpec((1, 1, d), lambda b, i: (b, 0, 0))
    return pl.pallas_call(
        _norm_kernel,
        grid=(bsz, s // t),
        in_specs=[
            pl.BlockSpec((1, t, d), lambda b, i: (b, i, 0)),
            pl.BlockSpec((1, d), lambda b, i: (0, 0)),
            vec, vec,
        ],
        out_specs=pl.BlockSpec((1, t, d), lambda b, i: (b, i, 0)),
        out_shape=jax.ShapeDtypeStruct((bsz, s, d), BF16),
        compiler_params=_params(("arbitrary", "arbitrary")),
        name="first_norm",
    )(x, gain.reshape(1, d), sc, sh)


def _group_scan(a, b):
    row = lax.broadcasted_iota(jnp.int32, a.shape, 0) % SUBLANES
    for d in (1, 2, 4):
        keep = row >= d
        a_prev = pltpu.roll(a, d, 0)
        b_prev = pltpu.roll(b, d, 0)
        b = jnp.where(keep, a * b_prev + b, b)
        a = jnp.where(keep, a * a_prev, a)
    return a, b


def _lru_kernel(h_ref, wx_ref, wg_ref, cw_ref, cb_ref, wab_ref, ba_ref, bx_ref, lam_ref,
                o_ref, tail_ref, hst_ref, a_scr, b_scr):
    i = pl.program_id(1)
    t = h_ref.shape[1]
    n_chunks = o_ref.shape[2] // CHUNK
    k_conv = cw_ref.shape[0]

    @pl.when(i == 0)
    def _():
        tail_ref[...] = jnp.zeros_like(tail_ref)
        hst_ref[...] = jnp.zeros_like(hst_ref)

    h = h_ref[0]
    first_row = (lax.broadcasted_iota(jnp.int32, (t, CHUNK), 0) + i * t) == 0
    for j in range(n_chunks):
        cs = slice(j * CHUNK, (j + 1) * CHUNK)
        ux = jnp.dot(h, wx_ref[:, cs], preferred_element_type=F32)
        ug = jnp.dot(h, wg_ref[:, cs], preferred_element_type=F32)
        ext = jnp.concatenate([tail_ref[:, cs], ux], axis=0)
        tail_ref[:, cs] = ux[t - SUBLANES:, :]
        xc = cb_ref[:, cs]
        for k in range(k_conv):
            xc = xc + cw_ref[k:k + 1, cs] * _shift_rows(ext, k_conv - 1 - k)[SUBLANES:, :]
        ri = jnp.dot(xc.astype(BF16), wab_ref[j], preferred_element_type=F32)
        r = _sigmoid(ri[:, :CHUNK] + ba_ref[:, cs])
        gate_i = _sigmoid(ri[:, CHUNK:] + bx_ref[:, cs])
        z = -lam_ref[:, cs]
        softplus = jnp.maximum(z, 0.0) + jnp.log1p(jnp.exp(-jnp.abs(z)))
        log_a = (-LRU_C * softplus) * r
        a = jnp.exp(log_a)
        th = jnp.tanh(log_a)
        mult = jnp.sqrt(-2.0 * th / (1.0 - th))
        mult = jnp.where(first_row, 1.0, mult)
        b = mult * (gate_i * xc)
        a, b = _group_scan(a, b)
        a_scr[...] = a
        b_scr[...] = b

        def carry_group(g, h_prev):
            r0 = pl.multiple_of(g * SUBLANES, SUBLANES)
            hg = a_scr[pl.ds(r0, SUBLANES), :] * h_prev + b_scr[pl.ds(r0, SUBLANES), :]
            b_scr[pl.ds(r0, SUBLANES), :] = hg
            return jnp.broadcast_to(hg[SUBLANES - 1:SUBLANES, :], (SUBLANES, CHUNK))

        h_last = lax.fori_loop(0, t // SUBLANES, carry_group, hst_ref[:, cs], unroll=8)
        hst_ref[:, cs] = h_last
        o_ref[0, :, cs] = (b_scr[...] * _gelu_tanh(ug)).astype(BF16)


def _lru_branch(h, w_in, col0, conv_w, conv_b, wab, b_a, b_x, lam):
    bsz, s, d = h.shape
    width = conv_w.shape[1]
    t = ROW_TILE
    cx, cg = col0 // width, col0 // width + 1
    full = lambda shape: pl.BlockSpec(shape, lambda b, i: (0,) * len(shape))
    return pl.pallas_call(
        _lru_kernel,
        grid=(bsz, s // t),
        in_specs=[
            pl.BlockSpec((1, t, d), lambda b, i: (b, i, 0)),
            pl.BlockSpec((d, width), lambda b, i: (0, cx)),
            pl.BlockSpec((d, width), lambda b, i: (0, cg)),
            full(conv_w.shape), full((1, width)), full(wab.shape),
            full((1, width)), full((1, width)), full((1, width)),
        ],
        out_specs=pl.BlockSpec((1, t, width), lambda b, i: (b, i, 0)),
        out_shape=jax.ShapeDtypeStruct((bsz, s, width), BF16),
        scratch_shapes=[
            pltpu.VMEM((SUBLANES, width), F32),
            pltpu.VMEM((SUBLANES, width), F32),
            pltpu.VMEM((t, CHUNK), F32),
            pltpu.VMEM((t, CHUNK), F32),
        ],
        compiler_params=_params(("arbitrary", "arbitrary")),
        name="lru_branch",
    )(h, w_in, w_in, conv_w, conv_b.reshape(1, width), wab,
      b_a.reshape(1, width), b_x.reshape(1, width), lam.reshape(1, width))


def _sc_kernel(h_ref, wb_ref, wc_ref, wx_ref, cw_ref, o_ref, tail_ref):
    i = pl.program_id(1)
    t = h_ref.shape[1]
    n_chunks = o_ref.shape[2] // CHUNK
    k_conv = cw_ref.shape[0]

    @pl.when(i == 0)
    def _():
        tail_ref[...] = jnp.zeros_like(tail_ref)

    h = h_ref[0]
    for j in range(n_chunks):
        cs = slice(j * CHUNK, (j + 1) * CHUNK)
        ub = jnp.dot(h, wb_ref[:, cs], preferred_element_type=F32)
        uc = jnp.dot(h, wc_ref[:, cs], preferred_element_type=F32)
        ux = jnp.dot(h, wx_ref[:, cs], preferred_element_type=F32)
        cx = uc * ux
        ext = jnp.concatenate([tail_ref[:, cs], cx], axis=0)
        tail_ref[:, cs] = cx[t - SUBLANES:, :]
        v = cw_ref[k_conv - 1:k_conv, cs] * cx
        for k in range(k_conv - 1):
            v = v + cw_ref[k:k + 1, cs] * _shift_rows(ext, k_conv - 1 - k)[SUBLANES:, :]
        o_ref[0, :, cs] = (ub * v).astype(BF16)


def _sc_branch(h, w_in, col0, conv_w):
    bsz, s, d = h.shape
    width = conv_w.shape[1]
    t = ROW_TILE
    c0 = col0 // width
    return pl.pallas_call(
        _sc_kernel,
        grid=(bsz, s // t),
        in_specs=[
            pl.BlockSpec((1, t, d), lambda b, i: (b, i, 0)),
            pl.BlockSpec((d, width), lambda b, i: (0, c0)),
            pl.BlockSpec((d, width), lambda b, i: (0, c0 + 1)),
            pl.BlockSpec((d, width), lambda b, i: (0, c0 + 2)),
            pl.BlockSpec(conv_w.shape, lambda b, i: (0, 0)),
        ],
        out_specs=pl.BlockSpec((1, t, width), lambda b, i: (b, i, 0)),
        out_shape=jax.ShapeDtypeStruct((bsz, s, width), BF16),
        scratch_shapes=[pltpu.VMEM((SUBLANES, width), F32)],
        compiler_params=_params(("arbitrary", "arbitrary")),
        name="sc_branch",
    )(h, w_in, w_in, w_in, conv_w)


POOL_TAIL = 16


def _pool_kernel(h_ref, wp_ref, o_ref, tail_ref):
    i = pl.program_id(1)
    t = h_ref.shape[1]
    group_dim = o_ref.shape[2] // len(POOL_WINDOWS)

    @pl.when(i == 0)
    def _():
        tail_ref[...] = jnp.zeros_like(tail_ref)

    h = h_ref[0]
    pos = lax.broadcasted_iota(jnp.int32, (t, group_dim), 0) + i * t
    for g, w in enumerate(POOL_WINDOWS):
        cs = slice(g * group_dim, (g + 1) * group_dim)
        u = jnp.dot(h, wp_ref[:, cs], preferred_element_type=F32)
        acc = jnp.concatenate([tail_ref[:, cs], u], axis=0)
        tail_ref[:, cs] = u[t - POOL_TAIL:, :]
        d = 1
        while d < w:
            acc = acc + pltpu.roll(acc, d, 0)
            d *= 2
        cnt = jnp.minimum(pos + 1, w).astype(F32)
        o_ref[0, :, cs] = (acc[POOL_TAIL:, :] / cnt - u).astype(BF16)


def _pool_branch(h, w_in, col0, width):
    bsz, s, d = h.shape
    t = ROW_TILE
    c0 = col0 // width
    return pl.pallas_call(
        _pool_kernel,
        grid=(bsz, s // t),
        in_specs=[
            pl.BlockSpec((1, t, d), lambda b, i: (b, i, 0)),
            pl.BlockSpec((d, width), lambda b, i: (0, c0)),
        ],
        out_specs=pl.BlockSpec((1, t, width), lambda b, i: (b, i, 0)),
        out_shape=jax.ShapeDtypeStruct((bsz, s, width), BF16),
        scratch_shapes=[pltpu.VMEM((POOL_TAIL, width), F32)],
        compiler_params=_params(("arbitrary", "arbitrary")),
        name="pool_branch",
    )(h, w_in)


def _merge_kernel(h_ref, ul_ref, us_ref, up_ref, wg0_ref, wg1_ref, wg2_ref,
                  wl_ref, ws_ref, wp_ref, ps_ref, o_ref):
    h = h_ref[...]
    dot = functools.partial(jnp.dot, preferred_element_type=F32)
    m = _sigmoid(dot(h, wg0_ref[...])) * dot(ul_ref[...], wl_ref[...])
    m = m + _sigmoid(dot(h, wg1_ref[...])) * dot(us_ref[...], ws_ref[...])
    m = m + _sigmoid(dot(h, wg2_ref[...])) * (dot(up_ref[...], wp_ref[0]) * ps_ref[...])
    o_ref[...] = m.astype(BF16)


def _merge(h, u_lru, u_sc, u_pool, w_in, gate_col0, w_lru_out, w_sc_out, pool_w, pool_scale):
    n, d = h.shape
    t, tn = MERGE_ROW_TILE, MERGE_COL_TILE
    n_groups, group_in, group_out = pool_w.shape
    assert group_out == tn
    g0 = gate_col0 // tn
    per_branch = d // tn
    rows = lambda width: pl.BlockSpec((t, width), lambda i, c: (i, 0))
    gate = lambda br: pl.BlockSpec((d, tn), lambda i, c: (0, g0 + br * per_branch + c))
    return pl.pallas_call(
        _merge_kernel,
        grid=(n // t, d // tn),
        in_specs=[
            rows(d), rows(u_lru.shape[1]), rows(u_sc.shape[1]),
            pl.BlockSpec((t, group_in), lambda i, c: (i, c)),
            gate(0), gate(1), gate(2),
            pl.BlockSpec((w_lru_out.shape[0], tn), lambda i, c: (0, c)),
            pl.BlockSpec((w_sc_out.shape[0], tn), lambda i, c: (0, c)),
            pl.BlockSpec((1, group_in, group_out), lambda i, c: (c, 0, 0)),
            pl.BlockSpec((1, tn), lambda i, c: (0, c)),
        ],
        out_specs=pl.BlockSpec((t, tn), lambda i, c: (i, c)),
        out_shape=jax.ShapeDtypeStruct((n, d), BF16),
        compiler_params=_params(("arbitrary", "arbitrary")),
        name="merge",
    )(h, u_lru, u_sc, u_pool, w_in, w_in, w_in, w_lru_out, w_sc_out, pool_w,
      pool_scale.reshape(1, d))


def _out_kernel(m_ref, w_ref, x_ref, g_ref, gain_ref, sc_ref, sh_ref, xo_ref, ho_ref):
    y = jnp.dot(m_ref[...], w_ref[...], preferred_element_type=F32)
    x_new = x_ref[...] + g_ref[0] * y
    xo_ref[...] = x_new
    ho_ref[...] = _mod_norm(x_new, gain_ref[...], sc_ref[0], sh_ref[0]).astype(BF16)


def _out_proj(merged, w_o, x, gate, gain, sc, sh, rows_per_batch):
    n, d = x.shape
    t = OUT_ROW_TILE
    tiles_per_batch = rows_per_batch // t
    rows = pl.BlockSpec((t, d), lambda i: (i, 0))
    vec = pl.BlockSpec((1, 1, d), lambda i: (i // tiles_per_batch, 0, 0))
    return pl.pallas_call(
        _out_kernel,
        grid=(n // t,),
        in_specs=[rows, pl.BlockSpec((d, d), lambda i: (0, 0)), rows, vec,
                  pl.BlockSpec((1, d), lambda i: (0, 0)), vec, vec],
        out_specs=[rows, rows],
        out_shape=[jax.ShapeDtypeStruct((n, d), F32), jax.ShapeDtypeStruct((n, d), BF16)],
        compiler_params=_params(("arbitrary",)),
        name="out_proj",
    )(merged, w_o, x, gate, gain.reshape(1, d), sc, sh)


def _ffn_kernel(h_ref, w1_ref, w3_ref, w2_ref, x_ref, g_ref, gain_ref, *rest, final):
    if final:
        xo_ref, acc_ref = rest
    else:
        sc_ref, sh_ref, xo_ref, ho_ref = rest
        acc_ref = xo_ref
    f = pl.program_id(1)
    h = h_ref[...]
    a = jnp.dot(h, w1_ref[...], preferred_element_type=F32)
    b = jnp.dot(h, w3_ref[...], preferred_element_type=F32)
    act = ((a * _sigmoid(a)) * b).astype(BF16)
    part = jnp.dot(act, w2_ref[...], preferred_element_type=F32)

    @pl.when(f == 0)
    def _():
        acc_ref[...] = part

    @pl.when(f > 0)
    def _():
        acc_ref[...] += part

    @pl.when(f == pl.num_programs(1) - 1)
    def _():
        x_new = x_ref[...] + g_ref[0] * acc_ref[...]
        if final:
            xo_ref[...] = x_new * lax.rsqrt(
                jnp.mean(x_new * x_new, axis=-1, keepdims=True) + EPS) * gain_ref[...]
        else:
            xo_ref[...] = x_new
            ho_ref[...] = _mod_norm(x_new, gain_ref[...], sc_ref[0], sh_ref[0]).astype(BF16)


def _ffn(h, w1, w3, w2, x, gate, gain, sc, sh, rows_per_batch, final):
    n, d = x.shape
    d_ff = w1.shape[1]
    t, tf = FFN_ROW_TILE, FFN_COL_TILE
    tiles_per_batch = rows_per_batch // t
    rows = pl.BlockSpec((t, d), lambda i, f: (i, 0))
    vec = pl.BlockSpec((1, 1, d), lambda i, f: (i // tiles_per_batch, 0, 0))
    in_specs = [rows,
                pl.BlockSpec((d, tf), lambda i, f: (0, f)),
                pl.BlockSpec((d, tf), lambda i, f: (0, f)),
                pl.BlockSpec((tf, d), lambda i, f: (f, 0)),
                rows, vec, pl.BlockSpec((1, d), lambda i, f: (0, 0))]
    args = [h, w1, w3, w2, x, gate, gain.reshape(1, d)]
    if final:
        out_specs = rows
        out_shape = jax.ShapeDtypeStruct((n, d), F32)
        scratch = [pltpu.VMEM((t, d), F32)]
    else:
        in_specs += [vec, vec]
        args += [sc, sh]
        out_specs = [rows, rows]
        out_shape = [jax.ShapeDtypeStruct((n, d), F32), jax.ShapeDtypeStruct((n, d), BF16)]
        scratch = []
    return pl.pallas_call(
        functools.partial(_ffn_kernel, final=final),
        grid=(n // t, d_ff // tf),
        in_specs=in_specs,
        out_specs=out_specs,
        out_shape=out_shape,
        scratch_shapes=scratch,
        compiler_params=_params(("arbitrary", "arbitrary")),
        name="ffn_final" if final else "ffn",
    )(*args)


def _block_diag_heads(w):
    depth, heads, hd, _ = w.shape
    per = CHUNK // hd
    w5 = w.reshape(depth, heads // per, per, hd, hd)
    eye = jnp.eye(per, dtype=w.dtype)
    bd = jnp.einsum("pq,ljpik->ljpiqk", eye, w5)
    return bd.reshape(depth, heads // per, CHUNK, CHUNK)


def kernel(x, c, w_mod, b_mod, norm_mix, w_in, lru_conv_w, lru_conv_b, lru_w_a, lru_b_a, lru_w_x, lru_b_x, lru_lambda, lru_w_out, sc_conv_w, sc_w_out, pool_w, pool_scale, w_o, norm_ffn, ffn_w1, ffn_w3, ffn_w2, norm_final):
    bsz, s, d = x.shape
    depth = w_in.shape[0]
    lru_width = lru_conv_w.shape[2]
    sc_width = sc_conv_w.shape[2]
    pool_width = pool_w.shape[1] * pool_w.shape[2]
    lru_col0 = 0
    sc_col0 = 2 * lru_width
    pool_col0 = sc_col0 + 3 * sc_width
    gate_col0 = pool_col0 + pool_width

    mod = _modulation(c, w_mod, b_mod)
    w_in_b = w_in.astype(BF16)
    wab = jnp.concatenate([_block_diag_heads(lru_w_a), _block_diag_heads(lru_w_x)],
                          axis=-1).astype(BF16)
    lru_w_out_b = lru_w_out.astype(BF16)
    sc_w_out_b = sc_w_out.astype(BF16)
    pool_w_b = pool_w.astype(BF16)
    w_o_b = w_o.astype(BF16)
    w1_b, w3_b, w2_b = ffn_w1.astype(BF16), ffn_w3.astype(BF16), ffn_w2.astype(BF16)

    sh1, sc1, g1, sh2, sc2, g2 = (mod[0, k] for k in range(N_MOD))
    h = _first_norm(x, norm_mix[0], sc1, sh1)
    xf = x.reshape(bsz * s, d)
    for l in range(depth):
        sh1, sc1, g1, sh2, sc2, g2 = (mod[l, k] for k in range(N_MOD))
        u_lru = _lru_branch(h, w_in_b[l], lru_col0, lru_conv_w[l], lru_conv_b[l], wab[l],
                            lru_b_a[l], lru_b_x[l], lru_lambda[l])
        u_sc = _sc_branch(h, w_in_b[l], sc_col0, sc_conv_w[l])
        u_pool = _pool_branch(h, w_in_b[l], pool_col0, pool_width)
        flat = lambda a: a.reshape(bsz * s, a.shape[-1])
        merged = _merge(flat(h), flat(u_lru), flat(u_sc), flat(u_pool), w_in_b[l], gate_col0,
                        lru_w_out_b[l], sc_w_out_b[l], pool_w_b[l], pool_scale[l])
        xf, h2 = _out_proj(merged, w_o_b[l], xf, g1, norm_ffn[l], sc2, sh2, s)
        if l + 1 < depth:
            nsh1, nsc1 = mod[l + 1, 0], mod[l + 1, 1]
            xf, hn = _ffn(h2, w1_b[l], w3_b[l], w2_b[l], xf, g2, norm_mix[l + 1], nsc1, nsh1,
                          s, final=False)
            h = hn.reshape(bsz, s, d)
        else:
            xf = _ffn(h2, w1_b[l], w3_b[l], w2_b[l], xf, g2, norm_final, None, None, s,
                      final=True)
    return xf.reshape(bsz, s, d)
```

```python
import functools

import jax
import jax.numpy as jnp
from jax import lax
from jax.experimental import pallas as pl
from jax.experimental.pallas import tpu as pltpu

F32 = jnp.float32
BF16 = jnp.bfloat16

EPS = 1e-6
LRU_C = 8.0
LRU_HEAD_DIM = 64
POOL_WINDOWS = (2, 4, 8, 16)
N_MOD = 6

SUBLANES = 8
CHUNK = 256
ROW_TILE = 512
MERGE_ROW_TILE = 1024
MERGE_COL_TILE = 512
OUT_ROW_TILE = 512
FFN_ROW_TILE = 512
FFN_COL_TILE = 512
MOD_COL_TILE = 1024
VMEM_LIMIT = 56 * 1024 * 1024


def _params(sem):
    return pltpu.CompilerParams(dimension_semantics=sem, vmem_limit_bytes=VMEM_LIMIT)


def _sigmoid(x):
    return 0.5 * jnp.tanh(0.5 * x) + 0.5


def _gelu_tanh(x):
    return 0.5 * x * (1.0 + jnp.tanh(0.7978845608028654 * (x + 0.044715 * (x * x * x))))


def _mod_norm(x, gain, scale, shift):
    y = x * lax.rsqrt(jnp.mean(x * x, axis=-1, keepdims=True) + EPS)
    return (y * gain) * (1.0 + scale) + shift


def _shift_rows(ext, s):
    if s == 0:
        return ext
    return pltpu.roll(ext, s, 0)


def _mod_kernel(c_ref, w_ref, b_ref, o_ref):
    c = c_ref[...]
    s = (c * _sigmoid(c)).astype(BF16)
    o_ref[0] = jnp.dot(s, w_ref[0].astype(BF16), preferred_element_type=F32) + b_ref[0]


def _modulation(c, w_mod, b_mod):
    depth, d, width = w_mod.shape
    bsz = c.shape[0]
    c_pad = jnp.zeros((SUBLANES, d), F32).at[:bsz].set(c)
    out = pl.pallas_call(
        _mod_kernel,
        grid=(depth, width // MOD_COL_TILE),
        in_specs=[
            pl.BlockSpec((SUBLANES, d), lambda l, n: (0, 0)),
            pl.BlockSpec((1, d, MOD_COL_TILE), lambda l, n: (l, 0, n)),
            pl.BlockSpec((1, 1, MOD_COL_TILE), lambda l, n: (l, 0, n)),
        ],
        out_specs=pl.BlockSpec((1, SUBLANES, MOD_COL_TILE), lambda l, n: (l, 0, n)),
        out_shape=jax.ShapeDtypeStruct((depth, SUBLANES, width), F32),
        compiler_params=_params(("arbitrary", "arbitrary")),
        name="modulation",
    )(c_pad, w_mod, b_mod.reshape(depth, 1, width))
    return out[:, :bsz].reshape(depth, bsz, N_MOD, 1, d).transpose(0, 2, 1, 3, 4)


def _norm_kernel(x_ref, gain_ref, sc_ref, sh_ref, o_ref):
    o_ref[0] = _mod_norm(x_ref[0], gain_ref[...], sc_ref[0], sh_ref[0]).astype(BF16)


def _first_norm(x, gain, sc, sh):
    bsz, s, d = x.shape
    t = ROW_TILE
    vec = pl.BlockSpec((1, 1, d), lambda b, i: (b, 0, 0))
    return pl.pallas_call(
        _norm_kernel,
        grid=(bsz, s // t),
        in_specs=[
            pl.BlockSpec((1, t, d), lambda b, i: (b, i, 0)),
            pl.BlockSpec((1, d), lambda b, i: (0, 0)),
            vec, vec,
        ],
        out_specs=pl.BlockSpec((1, t, d), lambda b, i: (b, i, 0)),
        out_shape=jax.ShapeDtypeStruct((bsz, s, d), BF16),
        compiler_params=_params(("arbitrary", "arbitrary")),
        name="first_norm",
    )(x, gain.reshape(1, d), sc, sh)


def _group_scan(a, b):
    row = lax.broadcasted_iota(jnp.int32, a.shape, 0) % SUBLANES
    for d in (1, 2, 4):
        keep = row >= d
        a_prev = pltpu.roll(a, d, 0)
        b_prev = pltpu.roll(b, d, 0)
        b = jnp.where(keep, a * b_prev + b, b)
        a = jnp.where(keep, a * a_prev, a)
    return a, b


def _lru_kernel(h_ref, wx_ref, wg_ref, cw_ref, cb_ref, wab_ref, ba_ref, bx_ref, lam_ref,
                o_ref, tail_ref, hst_ref, a_scr, b_scr):
    i = pl.program_id(1)
    t = h_ref.shape[1]
    n_chunks = o_ref.shape[2] // CHUNK
    k_conv = cw_ref.shape[0]

    @pl.when(i == 0)
    def _():
        tail_ref[...] = jnp.zeros_like(tail_ref)
        hst_ref[...] = jnp.zeros_like(hst_ref)

    h = h_ref[0]
    first_row = (lax.broadcasted_iota(jnp.int32, (t, CHUNK), 0) + i * t) == 0
    for j in range(n_chunks):
        cs = slice(j * CHUNK, (j + 1) * CHUNK)
        ux = jnp.dot(h, wx_ref[:, cs], preferred_element_type=F32)
        ug = jnp.dot(h, wg_ref[:, cs], preferred_element_type=F32)
        ext = jnp.concatenate([tail_ref[:, cs], ux], axis=0)
        tail_ref[:, cs] = ux[t - SUBLANES:, :]
        xc = cb_ref[:, cs]
        for k in range(k_conv):
            xc = xc + cw_ref[k:k + 1, cs] * _shift_rows(ext, k_conv - 1 - k)[SUBLANES:, :]
        ri = jnp.dot(xc.astype(BF16), wab_ref[j], preferred_element_type=F32)
        r = _sigmoid(ri[:, :CHUNK] + ba_ref[:, cs])
        gate_i = _sigmoid(ri[:, CHUNK:] + bx_ref[:, cs])
        z = -lam_ref[:, cs]
        softplus = jnp.maximum(z, 0.0) + jnp.log1p(jnp.exp(-jnp.abs(z)))
        log_a = (-LRU_C * softplus) * r
        a = jnp.exp(log_a)
        th = jnp.tanh(log_a)
        mult = jnp.sqrt(-2.0 * th / (1.0 - th))
        mult = jnp.where(first_row, 1.0, mult)
        b = mult * (gate_i * xc)
        a, b = _group_scan(a, b)
        a_scr[...] = a
        b_scr[...] = b

        def carry_group(g, h_prev):
            r0 = pl.multiple_of(g * SUBLANES, SUBLANES)
            hg = a_scr[pl.ds(r0, SUBLANES), :] * h_prev + b_scr[pl.ds(r0, SUBLANES), :]
            b_scr[pl.ds(r0, SUBLANES), :] = hg
            return jnp.broadcast_to(hg[SUBLANES - 1:SUBLANES, :], (SUBLANES, CHUNK))

        h_last = lax.fori_loop(0, t // SUBLANES, carry_group, hst_ref[:, cs], unroll=8)
        hst_ref[:, cs] = h_last
        o_ref[0, :, cs] = (b_scr[...] * _gelu_tanh(ug)).astype(BF16)


def _lru_branch(h, w_in, l, col0, conv_w, conv_b, wab, b_a, b_x, lam):
    bsz, s, d = h.shape
    width = conv_w.shape[1]
    t = ROW_TILE
    cx, cg = col0 // width, col0 // width + 1
    full = lambda shape: pl.BlockSpec(shape, lambda b, i: (0,) * len(shape))
    return pl.pallas_call(
        _lru_kernel,
        grid=(bsz, s // t),
        in_specs=[
            pl.BlockSpec((1, t, d), lambda b, i: (b, i, 0)),
            pl.BlockSpec((None, d, width), lambda b, i: (l, 0, cx)),
            pl.BlockSpec((None, d, width), lambda b, i: (l, 0, cg)),
            full(conv_w.shape), full((1, width)),
            pl.BlockSpec((None,) + wab.shape[1:], lambda b, i: (l, 0, 0, 0)),
            full((1, width)), full((1, width)), full((1, width)),
        ],
        out_specs=pl.BlockSpec((1, t, width), lambda b, i: (b, i, 0)),
        out_shape=jax.ShapeDtypeStruct((bsz, s, width), BF16),
        scratch_shapes=[
            pltpu.VMEM((SUBLANES, width), F32),
            pltpu.VMEM((SUBLANES, width), F32),
            pltpu.VMEM((t, CHUNK), F32),
            pltpu.VMEM((t, CHUNK), F32),
        ],
        compiler_params=_params(("arbitrary", "arbitrary")),
        name="lru_branch",
    )(h, w_in, w_in, conv_w, conv_b.reshape(1, width), wab,
      b_a.reshape(1, width), b_x.reshape(1, width), lam.reshape(1, width))


def _sc_kernel(h_ref, wb_ref, wc_ref, wx_ref, cw_ref, o_ref, tail_ref):
    i = pl.program_id(1)
    t = h_ref.shape[1]
    n_chunks = o_ref.shape[2] // CHUNK
    k_conv = cw_ref.shape[0]

    @pl.when(i == 0)
    def _():
        tail_ref[...] = jnp.zeros_like(tail_ref)

    h = h_ref[0]
    for j in range(n_chunks):
        cs = slice(j * CHUNK, (j + 1) * CHUNK)
        ub = jnp.dot(h, wb_ref[:, cs], preferred_element_type=F32)
        uc = jnp.dot(h, wc_ref[:, cs], preferred_element_type=F32)
        ux = jnp.dot(h, wx_ref[:, cs], preferred_element_type=F32)
        cx = uc * ux
        ext = jnp.concatenate([tail_ref[:, cs], cx], axis=0)
        tail_ref[:, cs] = cx[t - SUBLANES:, :]
        v = cw_ref[k_conv - 1:k_conv, cs] * cx
        for k in range(k_conv - 1):
            v = v + cw_ref[k:k + 1, cs] * _shift_rows(ext, k_conv - 1 - k)[SUBLANES:, :]
        o_ref[0, :, cs] = (ub * v).astype(BF16)


def _sc_branch(h, w_in, l, col0, conv_w):
    bsz, s, d = h.shape
    width = conv_w.shape[1]
    t = ROW_TILE
    c0 = col0 // width
    return pl.pallas_call(
        _sc_kernel,
        grid=(bsz, s // t),
        in_specs=[
            pl.BlockSpec((1, t, d), lambda b, i: (b, i, 0)),
            pl.BlockSpec((None, d, width), lambda b, i: (l, 0, c0)),
            pl.BlockSpec((None, d, width), lambda b, i: (l, 0, c0 + 1)),
            pl.BlockSpec((None, d, width), lambda b, i: (l, 0, c0 + 2)),
            pl.BlockSpec(conv_w.shape, lambda b, i: (0, 0)),
        ],
        out_specs=pl.BlockSpec((1, t, width), lambda b, i: (b, i, 0)),
        out_shape=jax.ShapeDtypeStruct((bsz, s, width), BF16),
        scratch_shapes=[pltpu.VMEM((SUBLANES, width), F32)],
        compiler_params=_params(("arbitrary", "arbitrary")),
        name="sc_branch",
    )(h, w_in, w_in, w_in, conv_w)


POOL_TAIL = 16


def _pool_kernel(h_ref, wp_ref, o_ref, tail_ref):
    i = pl.program_id(1)
    t = h_ref.shape[1]
    group_dim = o_ref.shape[2] // len(POOL_WINDOWS)

    @pl.when(i == 0)
    def _():
        tail_ref[...] = jnp.zeros_like(tail_ref)

    h = h_ref[0]
    pos = lax.broadcasted_iota(jnp.int32, (t, group_dim), 0) + i * t
    for g, w in enumerate(POOL_WINDOWS):
        cs = slice(g * group_dim, (g + 1) * group_dim)
        u = jnp.dot(h, wp_ref[:, cs], preferred_element_type=F32)
        acc = jnp.concatenate([tail_ref[:, cs], u], axis=0)
        tail_ref[:, cs] = u[t - POOL_TAIL:, :]
        d = 1
        while d < w:
            acc = acc + pltpu.roll(acc, d, 0)
            d *= 2
        cnt = jnp.minimum(pos + 1, w).astype(F32)
        o_ref[0, :, cs] = (acc[POOL_TAIL:, :] / cnt - u).astype(BF16)


def _pool_branch(h, w_in, l, col0, width):
    bsz, s, d = h.shape
    t = ROW_TILE
    c0 = col0 // width
    return pl.pallas_call(
        _pool_kernel,
        grid=(bsz, s // t),
        in_specs=[
            pl.BlockSpec((1, t, d), lambda b, i: (b, i, 0)),
            pl.BlockSpec((None, d, width), lambda b, i: (l, 0, c0)),
        ],
        out_specs=pl.BlockSpec((1, t, width), lambda b, i: (b, i, 0)),
        out_shape=jax.ShapeDtypeStruct((bsz, s, width), BF16),
        scratch_shapes=[pltpu.VMEM((POOL_TAIL, width), F32)],
        compiler_params=_params(("arbitrary", "arbitrary")),
        name="pool_branch",
    )(h, w_in)


def _merge_kernel(h_ref, ul_ref, us_ref, up_ref, wg0_ref, wg1_ref, wg2_ref,
                  wl_ref, ws_ref, wp_ref, ps_ref, o_ref):
    h = h_ref[...]
    dot = functools.partial(jnp.dot, preferred_element_type=F32)
    m = _sigmoid(dot(h, wg0_ref[...])) * dot(ul_ref[...], wl_ref[...])
    m = m + _sigmoid(dot(h, wg1_ref[...])) * dot(us_ref[...], ws_ref[...])
    m = m + _sigmoid(dot(h, wg2_ref[...])) * (dot(up_ref[...], wp_ref[0]) * ps_ref[...])
    o_ref[...] = m.astype(BF16)


def _merge(h, u_lru, u_sc, u_pool, w_in, l, gate_col0, w_lru_out, w_sc_out, pool_w, pool_scale):
    n, d = h.shape
    t, tn = MERGE_ROW_TILE, MERGE_COL_TILE
    _, n_groups, group_in, group_out = pool_w.shape
    assert group_out == tn
    g0 = gate_col0 // tn
    per_branch = d // tn
    rows = lambda width: pl.BlockSpec((t, width), lambda i, c: (i, 0))
    gate = lambda br: pl.BlockSpec((None, d, tn), lambda i, c: (l, 0, g0 + br * per_branch + c))
    return pl.pallas_call(
        _merge_kernel,
        grid=(n // t, d // tn),
        in_specs=[
            rows(d), rows(u_lru.shape[1]), rows(u_sc.shape[1]),
            pl.BlockSpec((t, group_in), lambda i, c: (i, c)),
            gate(0), gate(1), gate(2),
            pl.BlockSpec((None, w_lru_out.shape[1], tn), lambda i, c: (l, 0, c)),
            pl.BlockSpec((None, w_sc_out.shape[1], tn), lambda i, c: (l, 0, c)),
            pl.BlockSpec((None, 1, group_in, group_out), lambda i, c: (l, c, 0, 0)),
            pl.BlockSpec((1, tn), lambda i, c: (0, c)),
        ],
        out_specs=pl.BlockSpec((t, tn), lambda i, c: (i, c)),
        out_shape=jax.ShapeDtypeStruct((n, d), BF16),
        compiler_params=_params(("arbitrary", "arbitrary")),
        name="merge",
    )(h, u_lru, u_sc, u_pool, w_in, w_in, w_in, w_lru_out, w_sc_out, pool_w,
      pool_scale.reshape(1, d))


def _out_kernel(m_ref, w_ref, x_ref, g_ref, gain_ref, sc_ref, sh_ref, xo_ref, ho_ref):
    y = jnp.dot(m_ref[...], w_ref[...], preferred_element_type=F32)
    x_new = x_ref[...] + g_ref[0] * y
    xo_ref[...] = x_new
    ho_ref[...] = _mod_norm(x_new, gain_ref[...], sc_ref[0], sh_ref[0]).astype(BF16)


def _out_proj(merged, w_o, l, x, gate, gain, sc, sh, rows_per_batch):
    n, d = x.shape
    t = OUT_ROW_TILE
    tiles_per_batch = rows_per_batch // t
    rows = pl.BlockSpec((t, d), lambda i: (i, 0))
    vec = pl.BlockSpec((1, 1, d), lambda i: (i // tiles_per_batch, 0, 0))
    return pl.pallas_call(
        _out_kernel,
        grid=(n // t,),
        in_specs=[rows, pl.BlockSpec((None, d, d), lambda i: (l, 0, 0)), rows, vec,
                  pl.BlockSpec((1, d), lambda i: (0, 0)), vec, vec],
        out_specs=[rows, rows],
        out_shape=[jax.ShapeDtypeStruct((n, d), F32), jax.ShapeDtypeStruct((n, d), BF16)],
        compiler_params=_params(("arbitrary",)),
        name="out_proj",
    )(merged, w_o, x, gate, gain.reshape(1, d), sc, sh)


def _ffn_kernel(h_ref, w1_ref, w3_ref, w2_ref, x_ref, g_ref, gain_ref, *rest, final):
    if final:
        xo_ref, acc_ref = rest
    else:
        sc_ref, sh_ref, xo_ref, ho_ref = rest
        acc_ref = xo_ref
    f = pl.program_id(1)

    @pl.when(f == 0)
    def _():
        acc_ref[...] = jnp.zeros_like(acc_ref)

    h = h_ref[...]
    a = jnp.dot(h, w1_ref[...], preferred_element_type=F32)
    b = jnp.dot(h, w3_ref[...], preferred_element_type=F32)
    act = ((a * _sigmoid(a)) * b).astype(BF16)
    acc_ref[...] += jnp.dot(act, w2_ref[...], preferred_element_type=F32)

    @pl.when(f == pl.num_programs(1) - 1)
    def _():
        x_new = x_ref[...] + g_ref[0] * acc_ref[...]
        if final:
            xo_ref[...] = x_new * lax.rsqrt(
                jnp.mean(x_new * x_new, axis=-1, keepdims=True) + EPS) * gain_ref[...]
        else:
            xo_ref[...] = x_new
            ho_ref[...] = _mod_norm(x_new, gain_ref[...], sc_ref[0], sh_ref[0]).astype(BF16)


def _ffn(h, w1, w3, w2, l, x, gate, gain, sc, sh, rows_per_batch, final):
    n, d = x.shape
    d_ff = w1.shape[2]
    t, tf = FFN_ROW_TILE, FFN_COL_TILE
    tiles_per_batch = rows_per_batch // t
    rows = pl.BlockSpec((t, d), lambda i, f: (i, 0))
    vec = pl.BlockSpec((1, 1, d), lambda i, f: (i // tiles_per_batch, 0, 0))
    in_specs = [rows,
                pl.BlockSpec((None, d, tf), lambda i, f: (l, 0, f)),
                pl.BlockSpec((None, d, tf), lambda i, f: (l, 0, f)),
                pl.BlockSpec((None, tf, d), lambda i, f: (l, f, 0)),
                rows, vec, pl.BlockSpec((1, d), lambda i, f: (0, 0))]
    args = [h, w1, w3, w2, x, gate, gain.reshape(1, d)]
    if final:
        out_specs = rows
        out_shape = jax.ShapeDtypeStruct((n, d), F32)
        scratch = [pltpu.VMEM((t, d), F32)]
    else:
        in_specs += [vec, vec]
        args += [sc, sh]
        out_specs = [rows, rows]
        out_shape = [jax.ShapeDtypeStruct((n, d), F32), jax.ShapeDtypeStruct((n, d), BF16)]
        scratch = []
    return pl.pallas_call(
        functools.partial(_ffn_kernel, final=final),
        grid=(n // t, d_ff // tf),
        in_specs=in_specs,
        out_specs=out_specs,
        out_shape=out_shape,
        scratch_shapes=scratch,
        compiler_params=_params(("arbitrary", "arbitrary")),
        name="ffn_final" if final else "ffn",
    )(*args)


def _block_diag_heads(w):
    depth, heads, hd, _ = w.shape
    per = CHUNK // hd
    w5 = w.reshape(depth, heads // per, per, hd, hd)
    eye = jnp.eye(per, dtype=w.dtype)
    bd = jnp.einsum("pq,ljpik->ljpiqk", eye, w5)
    return bd.reshape(depth, heads // per, CHUNK, CHUNK)


def kernel(x, c, w_mod, b_mod, norm_mix, w_in, lru_conv_w, lru_conv_b, lru_w_a, lru_b_a, lru_w_x, lru_b_x, lru_lambda, lru_w_out, sc_conv_w, sc_w_out, pool_w, pool_scale, w_o, norm_ffn, ffn_w1, ffn_w3, ffn_w2, norm_final):
    bsz, s, d = x.shape
    depth = w_in.shape[0]
    lru_width = lru_conv_w.shape[2]
    sc_width = sc_conv_w.shape[2]
    pool_width = pool_w.shape[1] * pool_w.shape[2]
    lru_col0 = 0
    sc_col0 = 2 * lru_width
    pool_col0 = sc_col0 + 3 * sc_width
    gate_col0 = pool_col0 + pool_width

    mod = _modulation(c, w_mod, b_mod)
    w_in_b = w_in.astype(BF16)
    wab = jnp.concatenate([_block_diag_heads(lru_w_a), _block_diag_heads(lru_w_x)],
                          axis=-1).astype(BF16)
    lru_w_out_b = lru_w_out.astype(BF16)
    sc_w_out_b = sc_w_out.astype(BF16)
    pool_w_b = pool_w.astype(BF16)
    w_o_b = w_o.astype(BF16)
    w1_b, w3_b, w2_b = ffn_w1.astype(BF16), ffn_w3.astype(BF16), ffn_w2.astype(BF16)

    sh1, sc1, g1, sh2, sc2, g2 = (mod[0, k] for k in range(N_MOD))
    h = _first_norm(x, norm_mix[0], sc1, sh1)
    xf = x.reshape(bsz * s, d)
    for l in range(depth):
        sh1, sc1, g1, sh2, sc2, g2 = (mod[l, k] for k in range(N_MOD))
        u_lru = _lru_branch(h, w_in_b, l, lru_col0, lru_conv_w[l], lru_conv_b[l], wab,
                            lru_b_a[l], lru_b_x[l], lru_lambda[l])
        u_sc = _sc_branch(h, w_in_b, l, sc_col0, sc_conv_w[l])
        u_pool = _pool_branch(h, w_in_b, l, pool_col0, pool_width)
        flat = lambda a: a.reshape(bsz * s, a.shape[-1])
        merged = _merge(flat(h), flat(u_lru), flat(u_sc), flat(u_pool), w_in_b, l, gate_col0,
                        lru_w_out_b, sc_w_out_b, pool_w_b, pool_scale[l])
        xf, h2 = _out_proj(merged, w_o_b, l, xf, g1, norm_ffn[l], sc2, sh2, s)
        if l + 1 < depth:
            nsh1, nsc1 = mod[l + 1, 0], mod[l + 1, 1]
            xf, hn = _ffn(h2, w1_b, w3_b, w2_b, l, xf, g2, norm_mix[l + 1], nsc1, nsh1,
                          s, final=False)
            h = hn.reshape(bsz, s, d)
        else:
            xf = _ffn(h2, w1_b, w3_b, w2_b, l, xf, g2, norm_final, None, None, s,
                      final=True)
    return xf.reshape(bsz, s, d)
```

```python
import functools

import jax
import jax.numpy as jnp
from jax import lax
from jax.experimental import pallas as pl
from jax.experimental.pallas import tpu as pltpu

F32 = jnp.float32
BF16 = jnp.bfloat16

EPS = 1e-6
LRU_C = 8.0
LRU_HEAD_DIM = 64
POOL_WINDOWS = (2, 4, 8, 16)
N_MOD = 6

SUBLANES = 8
LANES = 128
CHUNK = 256
ROW_TILE = 512
MERGE_ROW_TILE = 1024
MERGE_COL_TILE = 512
OUT_ROW_TILE = 512
FFN_ROW_TILE = 512
FFN_COL_TILE = 512
MOD_COL_TILE = 1024
VMEM_LIMIT = 56 * 1024 * 1024


def _params(sem):
    return pltpu.CompilerParams(dimension_semantics=sem, vmem_limit_bytes=VMEM_LIMIT)


def _sigmoid(x):
    return 0.5 * jnp.tanh(0.5 * x) + 0.5


def _gelu_tanh(x):
    return 0.5 * x * (1.0 + jnp.tanh(0.7978845608028654 * (x + 0.044715 * (x * x * x))))


def _mod_norm(x, gain, scale, shift):
    y = x * lax.rsqrt(jnp.mean(x * x, axis=-1, keepdims=True) + EPS)
    return (y * gain) * (1.0 + scale) + shift


def _shift_rows(ext, s):
    if s == 0:
        return ext
    return pltpu.roll(ext, s, 0)


def _mod_kernel(c_ref, w_ref, b_ref, o_ref):
    c = c_ref[...]
    s = (c * _sigmoid(c)).astype(BF16)
    o_ref[0] = jnp.dot(s, w_ref[0].astype(BF16), preferred_element_type=F32) + b_ref[0]


def _modulation(c, w_mod, b_mod):
    depth, d, width = w_mod.shape
    bsz = c.shape[0]
    c_pad = jnp.zeros((SUBLANES, d), F32).at[:bsz].set(c)
    out = pl.pallas_call(
        _mod_kernel,
        grid=(depth, width // MOD_COL_TILE),
        in_specs=[
            pl.BlockSpec((SUBLANES, d), lambda l, n: (0, 0)),
            pl.BlockSpec((1, d, MOD_COL_TILE), lambda l, n: (l, 0, n)),
            pl.BlockSpec((1, 1, MOD_COL_TILE), lambda l, n: (l, 0, n)),
        ],
        out_specs=pl.BlockSpec((1, SUBLANES, MOD_COL_TILE), lambda l, n: (l, 0, n)),
        out_shape=jax.ShapeDtypeStruct((depth, SUBLANES, width), F32),
        compiler_params=_params(("arbitrary", "arbitrary")),
        name="modulation",
    )(c_pad, w_mod, b_mod.reshape(depth, 1, width))
    return out[:, :bsz].reshape(depth, bsz, N_MOD, 1, d).transpose(0, 2, 1, 3, 4)


def _norm_kernel(x_ref, gain_ref, sc_ref, sh_ref, o_ref):
    o_ref[0] = _mod_norm(x_ref[0], gain_ref[...], sc_ref[0], sh_ref[0]).astype(BF16)


def _first_norm(x, gain, sc, sh):
    bsz, s, d = x.shape
    t = ROW_TILE
    vec = pl.BlockSpec((1, 1, d), lambda b, i: (b, 0, 0))
    return pl.pallas_call(
        _norm_kernel,
        grid=(bsz, s // t),
        in_specs=[
            pl.BlockSpec((1, t, d), lambda b, i: (b, i, 0)),
            pl.BlockSpec((1, d), lambda b, i: (0, 0)),
            vec, vec,
        ],
        out_specs=pl.BlockSpec((1, t, d), lambda b, i: (b, i, 0)),
        out_shape=jax.ShapeDtypeStruct((bsz, s, d), BF16),
        compiler_params=_params(("arbitrary", "arbitrary")),
        name="first_norm",
    )(x, gain.reshape(1, d), sc, sh)


def _group_scan(a, b):
    row = lax.broadcasted_iota(jnp.int32, a.shape, 0) % SUBLANES
    for d in (1, 2, 4):
        keep = row >= d
        a_prev = pltpu.roll(a, d, 0)
        b_prev = pltpu.roll(b, d, 0)
        b = jnp.where(keep, a * b_prev + b, b)
        a = jnp.where(keep, a * a_prev, a)
    return a, b


def _lru_kernel(h_ref, wx_ref, wg_ref, cw_ref, cb_ref, wab_ref, ba_ref, bx_ref, lam_ref,
                o_ref, tail_ref, hst_ref, perm_ref):
    i = pl.program_id(1)
    t = h_ref.shape[1]
    n_blocks = SUBLANES
    g_rows = t // n_blocks
    n_chunks = o_ref.shape[2] // CHUNK
    n_slabs = CHUNK // LANES
    k_conv = cw_ref.shape[0]

    @pl.when(i == 0)
    def _():
        tail_ref[...] = jnp.zeros_like(tail_ref)
        hst_ref[...] = jnp.zeros_like(hst_ref)

    h = h_ref[0]
    row = lax.broadcasted_iota(jnp.int32, (g_rows, CHUNK), 0)
    is_row0 = row == 0
    seq_start = (row + i * g_rows) == 0

    def project(j):
        cols = slice(j * CHUNK, (j + 1) * CHUNK)
        return (jnp.dot(h, wx_ref[:, cols], preferred_element_type=F32),
                jnp.dot(h, wg_ref[:, cols], preferred_element_type=F32))

    projected = project(0)
    for j in range(n_chunks):
        cs = slice(j * CHUNK, (j + 1) * CHUNK)
        ux, ug = projected
        if j + 1 < n_chunks:
            projected = project(j + 1)
        prev_tail = tail_ref[:, cs]
        tail_ref[:, cs] = ux[t - SUBLANES:, :]
        p_in, p_out = perm_ref.at[2 * j], perm_ref.at[2 * j + 1]
        for s in range(n_slabs):
            p_in[s] = ux[:, s * LANES:(s + 1) * LANES]
        x = [jnp.concatenate([p_in[s, pl.ds(r, g_rows, stride=n_blocks), :]
                              for s in range(n_slabs)], axis=1) for r in range(n_blocks)]
        late = {q: jnp.where(is_row0, prev_tail[q:q + 1, :], pltpu.roll(x[q], 1, 0))
                for q in range(n_blocks - (k_conv - 1), n_blocks)}
        xc = []
        for r in range(n_blocks):
            acc = cb_ref[:, cs]
            for k in range(k_conv):
                s = k_conv - 1 - k
                src = x[r - s] if r >= s else late[r - s + n_blocks]
                acc = acc + cw_ref[k:k + 1, cs] * src
            xc.append(acc)
        ri = jnp.dot(jnp.concatenate(xc, axis=0).astype(BF16), wab_ref[j],
                     preferred_element_type=F32)
        z = -lam_ref[:, cs]
        neg_c_softplus = -LRU_C * (jnp.maximum(z, 0.0) + jnp.log1p(jnp.exp(-jnp.abs(z))))
        hs, ps = [], []
        for r in range(n_blocks):
            rs = slice(r * g_rows, (r + 1) * g_rows)
            gate_r = _sigmoid(ri[rs, :CHUNK] + ba_ref[:, cs])
            gate_i = _sigmoid(ri[rs, CHUNK:] + bx_ref[:, cs])
            log_a = neg_c_softplus * gate_r
            a = jnp.exp(log_a)
            th = jnp.tanh(log_a)
            mult = jnp.sqrt(-2.0 * th / (1.0 - th))
            if r == 0:
                mult = jnp.where(seq_start, 1.0, mult)
            b = mult * (gate_i * xc[r])
            hs.append(b if r == 0 else a * hs[-1] + b)
            ps.append(a if r == 0 else a * ps[-1])
        a2, b2 = _group_scan(ps[-1], hs[-1])
        state0 = hst_ref[:, cs]
        carry = state0
        ends = []
        for k in range(g_rows // SUBLANES):
            ks = slice(k * SUBLANES, (k + 1) * SUBLANES)
            end_k = a2[ks, :] * carry + b2[ks, :]
            ends.append(end_k)
            carry = jnp.broadcast_to(end_k[SUBLANES - 1:SUBLANES, :], (SUBLANES, CHUNK))
        hst_ref[:, cs] = carry
        h_in = jnp.where(is_row0, state0[0:1, :], pltpu.roll(jnp.concatenate(ends, axis=0), 1, 0))
        for r in range(n_blocks):
            h_r = hs[r] + ps[r] * h_in
            for s in range(n_slabs):
                p_out[s, pl.ds(r, g_rows, stride=n_blocks), :] = h_r[:, s * LANES:(s + 1) * LANES]
        h_time = jnp.concatenate([p_out[s] for s in range(n_slabs)], axis=1)
        o_ref[0, :, cs] = (h_time * _gelu_tanh(ug)).astype(BF16)


def _lru_branch(h, w_in, l, col0, conv_w, conv_b, wab, b_a, b_x, lam):
    bsz, s, d = h.shape
    width = conv_w.shape[1]
    t = ROW_TILE
    cx, cg = col0 // width, col0 // width + 1
    full = lambda shape: pl.BlockSpec(shape, lambda b, i: (0,) * len(shape))
    return pl.pallas_call(
        _lru_kernel,
        grid=(bsz, s // t),
        in_specs=[
            pl.BlockSpec((1, t, d), lambda b, i: (b, i, 0)),
            pl.BlockSpec((None, d, width), lambda b, i: (l, 0, cx)),
            pl.BlockSpec((None, d, width), lambda b, i: (l, 0, cg)),
            full(conv_w.shape), full((1, width)),
            pl.BlockSpec((None,) + wab.shape[1:], lambda b, i: (l, 0, 0, 0)),
            full((1, width)), full((1, width)), full((1, width)),
        ],
        out_specs=pl.BlockSpec((1, t, width), lambda b, i: (b, i, 0)),
        out_shape=jax.ShapeDtypeStruct((bsz, s, width), BF16),
        scratch_shapes=[
            pltpu.VMEM((SUBLANES, width), F32),
            pltpu.VMEM((SUBLANES, width), F32),
            pltpu.VMEM((2 * width // CHUNK, CHUNK // LANES, t, LANES), F32),
        ],
        compiler_params=_params(("arbitrary", "arbitrary")),
        name="lru_branch",
    )(h, w_in, w_in, conv_w, conv_b.reshape(1, width), wab,
      b_a.reshape(1, width), b_x.reshape(1, width), lam.reshape(1, width))


def _sc_kernel(h_ref, wb_ref, wc_ref, wx_ref, cw_ref, o_ref, tail_ref):
    i = pl.program_id(1)
    t = h_ref.shape[1]
    n_chunks = o_ref.shape[2] // CHUNK
    k_conv = cw_ref.shape[0]

    @pl.when(i == 0)
    def _():
        tail_ref[...] = jnp.zeros_like(tail_ref)

    h = h_ref[0]
    for j in range(n_chunks):
        cs = slice(j * CHUNK, (j + 1) * CHUNK)
        ub = jnp.dot(h, wb_ref[:, cs], preferred_element_type=F32)
        uc = jnp.dot(h, wc_ref[:, cs], preferred_element_type=F32)
        ux = jnp.dot(h, wx_ref[:, cs], preferred_element_type=F32)
        cx = uc * ux
        ext = jnp.concatenate([tail_ref[:, cs], cx], axis=0)
        tail_ref[:, cs] = cx[t - SUBLANES:, :]
        v = cw_ref[k_conv - 1:k_conv, cs] * cx
        for k in range(k_conv - 1):
            v = v + cw_ref[k:k + 1, cs] * _shift_rows(ext, k_conv - 1 - k)[SUBLANES:, :]
        o_ref[0, :, cs] = (ub * v).astype(BF16)


def _sc_branch(h, w_in, l, col0, conv_w):
    bsz, s, d = h.shape
    width = conv_w.shape[1]
    t = ROW_TILE
    c0 = col0 // width
    return pl.pallas_call(
        _sc_kernel,
        grid=(bsz, s // t),
        in_specs=[
            pl.BlockSpec((1, t, d), lambda b, i: (b, i, 0)),
            pl.BlockSpec((None, d, width), lambda b, i: (l, 0, c0)),
            pl.BlockSpec((None, d, width), lambda b, i: (l, 0, c0 + 1)),
            pl.BlockSpec((None, d, width), lambda b, i: (l, 0, c0 + 2)),
            pl.BlockSpec(conv_w.shape, lambda b, i: (0, 0)),
        ],
        out_specs=pl.BlockSpec((1, t, width), lambda b, i: (b, i, 0)),
        out_shape=jax.ShapeDtypeStruct((bsz, s, width), BF16),
        scratch_shapes=[pltpu.VMEM((SUBLANES, width), F32)],
        compiler_params=_params(("arbitrary", "arbitrary")),
        name="sc_branch",
    )(h, w_in, w_in, w_in, conv_w)


POOL_TAIL = 16


def _pool_kernel(h_ref, wp_ref, o_ref, tail_ref):
    i = pl.program_id(1)
    t = h_ref.shape[1]
    group_dim = o_ref.shape[2] // len(POOL_WINDOWS)

    @pl.when(i == 0)
    def _():
        tail_ref[...] = jnp.zeros_like(tail_ref)

    h = h_ref[0]
    pos = lax.broadcasted_iota(jnp.int32, (t, group_dim), 0) + i * t
    for g, w in enumerate(POOL_WINDOWS):
        cs = slice(g * group_dim, (g + 1) * group_dim)
        u = jnp.dot(h, wp_ref[:, cs], preferred_element_type=F32)
        acc = jnp.concatenate([tail_ref[:, cs], u], axis=0)
        tail_ref[:, cs] = u[t - POOL_TAIL:, :]
        d = 1
        while d < w:
            acc = acc + pltpu.roll(acc, d, 0)
            d *= 2
        cnt = jnp.minimum(pos + 1, w).astype(F32)
        o_ref[0, :, cs] = (acc[POOL_TAIL:, :] / cnt - u).astype(BF16)


def _pool_branch(h, w_in, l, col0, width):
    bsz, s, d = h.shape
    t = ROW_TILE
    c0 = col0 // width
    return pl.pallas_call(
        _pool_kernel,
        grid=(bsz, s // t),
        in_specs=[
            pl.BlockSpec((1, t, d), lambda b, i: (b, i, 0)),
            pl.BlockSpec((None, d, width), lambda b, i: (l, 0, c0)),
        ],
        out_specs=pl.BlockSpec((1, t, width), lambda b, i: (b, i, 0)),
        out_shape=jax.ShapeDtypeStruct((bsz, s, width), BF16),
        scratch_shapes=[pltpu.VMEM((POOL_TAIL, width), F32)],
        compiler_params=_params(("arbitrary", "arbitrary")),
        name="pool_branch",
    )(h, w_in)


def _merge_kernel(h_ref, ul_ref, us_ref, up_ref, wg0_ref, wg1_ref, wg2_ref,
                  wl_ref, ws_ref, wp_ref, ps_ref, o_ref):
    h = h_ref[...]
    dot = functools.partial(jnp.dot, preferred_element_type=F32)
    m = _sigmoid(dot(h, wg0_ref[...])) * dot(ul_ref[...], wl_ref[...])
    m = m + _sigmoid(dot(h, wg1_ref[...])) * dot(us_ref[...], ws_ref[...])
    m = m + _sigmoid(dot(h, wg2_ref[...])) * (dot(up_ref[...], wp_ref[0]) * ps_ref[...])
    o_ref[...] = m.astype(BF16)


def _merge(h, u_lru, u_sc, u_pool, w_in, l, gate_col0, w_lru_out, w_sc_out, pool_w, pool_scale):
    n, d = h.shape
    t, tn = MERGE_ROW_TILE, MERGE_COL_TILE
    _, n_groups, group_in, group_out = pool_w.shape
    assert group_out == tn
    g0 = gate_col0 // tn
    per_branch = d // tn
    rows = lambda width: pl.BlockSpec((t, width), lambda i, c: (i, 0))
    gate = lambda br: pl.BlockSpec((None, d, tn), lambda i, c: (l, 0, g0 + br * per_branch + c))
    return pl.pallas_call(
        _merge_kernel,
        grid=(n // t, d // tn),
        in_specs=[
            rows(d), rows(u_lru.shape[1]), rows(u_sc.shape[1]),
            pl.BlockSpec((t, group_in), lambda i, c: (i, c)),
            gate(0), gate(1), gate(2),
            pl.BlockSpec((None, w_lru_out.shape[1], tn), lambda i, c: (l, 0, c)),
            pl.BlockSpec((None, w_sc_out.shape[1], tn), lambda i, c: (l, 0, c)),
            pl.BlockSpec((None, 1, group_in, group_out), lambda i, c: (l, c, 0, 0)),
            pl.BlockSpec((1, tn), lambda i, c: (0, c)),
        ],
        out_specs=pl.BlockSpec((t, tn), lambda i, c: (i, c)),
        out_shape=jax.ShapeDtypeStruct((n, d), BF16),
        compiler_params=_params(("arbitrary", "arbitrary")),
        name="merge",
    )(h, u_lru, u_sc, u_pool, w_in, w_in, w_in, w_lru_out, w_sc_out, pool_w,
      pool_scale.reshape(1, d))


def _out_kernel(m_ref, w_ref, x_ref, g_ref, gain_ref, sc_ref, sh_ref, xo_ref, ho_ref):
    y = jnp.dot(m_ref[...], w_ref[...], preferred_element_type=F32)
    x_new = x_ref[...] + g_ref[0] * y
    xo_ref[...] = x_new
    ho_ref[...] = _mod_norm(x_new, gain_ref[...], sc_ref[0], sh_ref[0]).astype(BF16)


def _out_proj(merged, w_o, l, x, gate, gain, sc, sh, rows_per_batch):
    n, d = x.shape
    t = OUT_ROW_TILE
    tiles_per_batch = rows_per_batch // t
    rows = pl.BlockSpec((t, d), lambda i: (i, 0))
    vec = pl.BlockSpec((1, 1, d), lambda i: (i // tiles_per_batch, 0, 0))
    return pl.pallas_call(
        _out_kernel,
        grid=(n // t,),
        in_specs=[rows, pl.BlockSpec((None, d, d), lambda i: (l, 0, 0)), rows, vec,
                  pl.BlockSpec((1, d), lambda i: (0, 0)), vec, vec],
        out_specs=[rows, rows],
        out_shape=[jax.ShapeDtypeStruct((n, d), F32), jax.ShapeDtypeStruct((n, d), BF16)],
        compiler_params=_params(("arbitrary",)),
        name="out_proj",
    )(merged, w_o, x, gate, gain.reshape(1, d), sc, sh)


def _ffn_kernel(h_ref, w1_ref, w3_ref, w2_ref, x_ref, g_ref, gain_ref, *rest, final):
    if final:
        xo_ref, acc_ref = rest
    else:
        sc_ref, sh_ref, xo_ref, ho_ref = rest
        acc_ref = xo_ref
    f = pl.program_id(1)

    @pl.when(f == 0)
    def _():
        acc_ref[...] = jnp.zeros_like(acc_ref)

    h = h_ref[...]
    a = jnp.dot(h, w1_ref[...], preferred_element_type=F32)
    b = jnp.dot(h, w3_ref[...], preferred_element_type=F32)
    act = ((a * _sigmoid(a)) * b).astype(BF16)
    acc_ref[...] += jnp.dot(act, w2_ref[...], preferred_element_type=F32)

    @pl.when(f == pl.num_programs(1) - 1)
    def _():
        x_new = x_ref[...] + g_ref[0] * acc_ref[...]
        if final:
            xo_ref[...] = x_new * lax.rsqrt(
                jnp.mean(x_new * x_new, axis=-1, keepdims=True) + EPS) * gain_ref[...]
        else:
            xo_ref[...] = x_new
            ho_ref[...] = _mod_norm(x_new, gain_ref[...], sc_ref[0], sh_ref[0]).astype(BF16)


def _ffn(h, w1, w3, w2, l, x, gate, gain, sc, sh, rows_per_batch, final):
    n, d = x.shape
    d_ff = w1.shape[2]
    t, tf = FFN_ROW_TILE, FFN_COL_TILE
    tiles_per_batch = rows_per_batch // t
    rows = pl.BlockSpec((t, d), lambda i, f: (i, 0))
    vec = pl.BlockSpec((1, 1, d), lambda i, f: (i // tiles_per_batch, 0, 0))
    in_specs = [rows,
                pl.BlockSpec((None, d, tf), lambda i, f: (l, 0, f)),
                pl.BlockSpec((None, d, tf), lambda i, f: (l, 0, f)),
                pl.BlockSpec((None, tf, d), lambda i, f: (l, f, 0)),
                rows, vec, pl.BlockSpec((1, d), lambda i, f: (0, 0))]
    args = [h, w1, w3, w2, x, gate, gain.reshape(1, d)]
    if final:
        out_specs = rows
        out_shape = jax.ShapeDtypeStruct((n, d), F32)
        scratch = [pltpu.VMEM((t, d), F32)]
    else:
        in_specs += [vec, vec]
        args += [sc, sh]
        out_specs = [rows, rows]
        out_shape = [jax.ShapeDtypeStruct((n, d), F32), jax.ShapeDtypeStruct((n, d), BF16)]
        scratch = []
    return pl.pallas_call(
        functools.partial(_ffn_kernel, final=final),
        grid=(n // t, d_ff // tf),
        in_specs=in_specs,
        out_specs=out_specs,
        out_shape=out_shape,
        scratch_shapes=scratch,
        compiler_params=_params(("arbitrary", "arbitrary")),
        name="ffn_final" if final else "ffn",
    )(*args)


def _block_diag_heads(w):
    depth, heads, hd, _ = w.shape
    per = CHUNK // hd
    w5 = w.reshape(depth, heads // per, per, hd, hd)
    eye = jnp.eye(per, dtype=w.dtype)
    bd = jnp.einsum("pq,ljpik->ljpiqk", eye, w5)
    return bd.reshape(depth, heads // per, CHUNK, CHUNK)


def kernel(x, c, w_mod, b_mod, norm_mix, w_in, lru_conv_w, lru_conv_b, lru_w_a, lru_b_a, lru_w_x, lru_b_x, lru_lambda, lru_w_out, sc_conv_w, sc_w_out, pool_w, pool_scale, w_o, norm_ffn, ffn_w1, ffn_w3, ffn_w2, norm_final):
    bsz, s, d = x.shape
    depth = w_in.shape[0]
    lru_width = lru_conv_w.shape[2]
    sc_width = sc_conv_w.shape[2]
    pool_width = pool_w.shape[1] * pool_w.shape[2]
    lru_col0 = 0
    sc_col0 = 2 * lru_width
    pool_col0 = sc_col0 + 3 * sc_width
    gate_col0 = pool_col0 + pool_width

    mod = _modulation(c, w_mod, b_mod)
    w_in_b = w_in.astype(BF16)
    wab = jnp.concatenate([_block_diag_heads(lru_w_a), _block_diag_heads(lru_w_x)],
                          axis=-1).astype(BF16)
    lru_w_out_b = lru_w_out.astype(BF16)
    sc_w_out_b = sc_w_out.astype(BF16)
    pool_w_b = pool_w.astype(BF16)
    w_o_b = w_o.astype(BF16)
    w1_b, w3_b, w2_b = ffn_w1.astype(BF16), ffn_w3.astype(BF16), ffn_w2.astype(BF16)

    sh1, sc1, g1, sh2, sc2, g2 = (mod[0, k] for k in range(N_MOD))
    h = _first_norm(x, norm_mix[0], sc1, sh1)
    xf = x.reshape(bsz * s, d)
    for l in range(depth):
        sh1, sc1, g1, sh2, sc2, g2 = (mod[l, k] for k in range(N_MOD))
        u_lru = _lru_branch(h, w_in_b, l, lru_col0, lru_conv_w[l], lru_conv_b[l], wab,
                            lru_b_a[l], lru_b_x[l], lru_lambda[l])
        u_sc = _sc_branch(h, w_in_b, l, sc_col0, sc_conv_w[l])
        u_pool = _pool_branch(h, w_in_b, l, pool_col0, pool_width)
        flat = lambda a: a.reshape(bsz * s, a.shape[-1])
        merged = _merge(flat(h), flat(u_lru), flat(u_sc), flat(u_pool), w_in_b, l, gate_col0,
                        lru_w_out_b, sc_w_out_b, pool_w_b, pool_scale[l])
        xf, h2 = _out_proj(merged, w_o_b, l, xf, g1, norm_ffn[l], sc2, sh2, s)
        if l + 1 < depth:
            nsh1, nsc1 = mod[l + 1, 0], mod[l + 1, 1]
            xf, hn = _ffn(h2, w1_b, w3_b, w2_b, l, xf, g2, norm_mix[l + 1], nsc1, nsh1,
                          s, final=False)
            h = hn.reshape(bsz, s, d)
        else:
            xf = _ffn(h2, w1_b, w3_b, w2_b, l, xf, g2, norm_final, None, None, s,
                      final=True)
    return xf.reshape(bsz, s, d)
```

```python
import functools

import jax
import jax.numpy as jnp
from jax import lax
from jax.experimental import pallas as pl
from jax.experimental.pallas import tpu as pltpu

F32 = jnp.float32
BF16 = jnp.bfloat16

EPS = 1e-6
LRU_C = 8.0
LRU_HEAD_DIM = 64
POOL_WINDOWS = (2, 4, 8, 16)
N_MOD = 6

SUBLANES = 8
LANES = 128
CHUNK = 256
ROW_TILE = 512
MERGE_ROW_TILE = 1024
MERGE_COL_TILE = 512
OUT_ROW_TILE = 512
FFN_ROW_TILE = 1024
FFN_COL_TILE = 512
MOD_COL_TILE = 1024
VMEM_LIMIT = 56 * 1024 * 1024


def _params(sem):
    return pltpu.CompilerParams(dimension_semantics=sem, vmem_limit_bytes=VMEM_LIMIT)


def _sigmoid(x):
    return 0.5 * jnp.tanh(0.5 * x) + 0.5


def _gelu_tanh(x):
    return 0.5 * x * (1.0 + jnp.tanh(0.7978845608028654 * (x + 0.044715 * (x * x * x))))


def _mod_norm(x, gain, scale, shift):
    y = x * lax.rsqrt(jnp.mean(x * x, axis=-1, keepdims=True) + EPS)
    return (y * gain) * (1.0 + scale) + shift


def _shift_rows(ext, s):
    if s == 0:
        return ext
    return pltpu.roll(ext, s, 0)


def _mod_kernel(c_ref, w_ref, b_ref, o_ref):
    c = c_ref[...]
    s = (c * _sigmoid(c)).astype(BF16)
    o_ref[0] = jnp.dot(s, w_ref[0].astype(BF16), preferred_element_type=F32) + b_ref[0]


def _modulation(c, w_mod, b_mod):
    depth, d, width = w_mod.shape
    bsz = c.shape[0]
    c_pad = jnp.zeros((SUBLANES, d), F32).at[:bsz].set(c)
    out = pl.pallas_call(
        _mod_kernel,
        grid=(depth, width // MOD_COL_TILE),
        in_specs=[
            pl.BlockSpec((SUBLANES, d), lambda l, n: (0, 0)),
            pl.BlockSpec((1, d, MOD_COL_TILE), lambda l, n: (l, 0, n)),
            pl.BlockSpec((1, 1, MOD_COL_TILE), lambda l, n: (l, 0, n)),
        ],
        out_specs=pl.BlockSpec((1, SUBLANES, MOD_COL_TILE), lambda l, n: (l, 0, n)),
        out_shape=jax.ShapeDtypeStruct((depth, SUBLANES, width), F32),
        compiler_params=_params(("arbitrary", "arbitrary")),
        name="modulation",
    )(c_pad, w_mod, b_mod.reshape(depth, 1, width))
    return out[:, :bsz].reshape(depth, bsz, N_MOD, 1, d).transpose(0, 2, 1, 3, 4)


def _norm_kernel(x_ref, gain_ref, sc_ref, sh_ref, o_ref):
    o_ref[0] = _mod_norm(x_ref[0], gain_ref[...], sc_ref[0], sh_ref[0]).astype(BF16)


def _first_norm(x, gain, sc, sh):
    bsz, s, d = x.shape
    t = ROW_TILE
    vec = pl.BlockSpec((1, 1, d), lambda b, i: (b, 0, 0))
    return pl.pallas_call(
        _norm_kernel,
        grid=(bsz, s // t),
        in_specs=[
            pl.BlockSpec((1, t, d), lambda b, i: (b, i, 0)),
            pl.BlockSpec((1, d), lambda b, i: (0, 0)),
            vec, vec,
        ],
        out_specs=pl.BlockSpec((1, t, d), lambda b, i: (b, i, 0)),
        out_shape=jax.ShapeDtypeStruct((bsz, s, d), BF16),
        compiler_params=_params(("arbitrary", "arbitrary")),
        name="first_norm",
    )(x, gain.reshape(1, d), sc, sh)


def _group_scan(a, b):
    row = lax.broadcasted_iota(jnp.int32, a.shape, 0) % SUBLANES
    for d in (1, 2, 4):
        keep = row >= d
        a_prev = pltpu.roll(a, d, 0)
        b_prev = pltpu.roll(b, d, 0)
        b = jnp.where(keep, a * b_prev + b, b)
        a = jnp.where(keep, a * a_prev, a)
    return a, b


def _lru_kernel(h_ref, wx_ref, wg_ref, cw_ref, cb_ref, wab_ref, ba_ref, bx_ref, lam_ref,
                o_ref, tail_ref, hst_ref, perm_ref):
    i = pl.program_id(1)
    t = h_ref.shape[1]
    n_blocks = SUBLANES
    g_rows = t // n_blocks
    n_chunks = o_ref.shape[2] // CHUNK
    n_slabs = CHUNK // LANES
    k_conv = cw_ref.shape[0]

    @pl.when(i == 0)
    def _():
        tail_ref[...] = jnp.zeros_like(tail_ref)
        hst_ref[...] = jnp.zeros_like(hst_ref)

    h = h_ref[0]
    row = lax.broadcasted_iota(jnp.int32, (g_rows, CHUNK), 0)
    is_row0 = row == 0
    seq_start = (row + i * g_rows) == 0

    def project(j):
        cols = slice(j * CHUNK, (j + 1) * CHUNK)
        return (jnp.dot(h, wx_ref[:, cols], preferred_element_type=F32),
                jnp.dot(h, wg_ref[:, cols], preferred_element_type=F32))

    projected = project(0)
    for j in range(n_chunks):
        cs = slice(j * CHUNK, (j + 1) * CHUNK)
        ux, ug = projected
        if j + 1 < n_chunks:
            projected = project(j + 1)
        prev_tail = tail_ref[:, cs]
        tail_ref[:, cs] = ux[t - SUBLANES:, :]
        p_in, p_out = perm_ref.at[2 * j], perm_ref.at[2 * j + 1]
        for s in range(n_slabs):
            p_in[s] = ux[:, s * LANES:(s + 1) * LANES]
        x = [jnp.concatenate([p_in[s, pl.ds(r, g_rows, stride=n_blocks), :]
                              for s in range(n_slabs)], axis=1) for r in range(n_blocks)]
        late = {q: jnp.where(is_row0, prev_tail[q:q + 1, :], pltpu.roll(x[q], 1, 0))
                for q in range(n_blocks - (k_conv - 1), n_blocks)}
        xc = []
        for r in range(n_blocks):
            acc = cb_ref[:, cs]
            for k in range(k_conv):
                s = k_conv - 1 - k
                src = x[r - s] if r >= s else late[r - s + n_blocks]
                acc = acc + cw_ref[k:k + 1, cs] * src
            xc.append(acc)
        ri = jnp.dot(jnp.concatenate(xc, axis=0).astype(BF16), wab_ref[j],
                     preferred_element_type=F32)
        z = -lam_ref[:, cs]
        neg_c_softplus = -LRU_C * (jnp.maximum(z, 0.0) + jnp.log1p(jnp.exp(-jnp.abs(z))))
        hs, ps = [], []
        for r in range(n_blocks):
            rs = slice(r * g_rows, (r + 1) * g_rows)
            gate_r = _sigmoid(ri[rs, :CHUNK] + ba_ref[:, cs])
            gate_i = _sigmoid(ri[rs, CHUNK:] + bx_ref[:, cs])
            log_a = neg_c_softplus * gate_r
            a = jnp.exp(log_a)
            th = jnp.tanh(log_a)
            mult = jnp.sqrt(-2.0 * th / (1.0 - th))
            if r == 0:
                mult = jnp.where(seq_start, 1.0, mult)
            b = mult * (gate_i * xc[r])
            hs.append(b if r == 0 else a * hs[-1] + b)
            ps.append(a if r == 0 else a * ps[-1])
        a2, b2 = _group_scan(ps[-1], hs[-1])
        state0 = hst_ref[:, cs]
        carry = state0
        ends = []
        for k in range(g_rows // SUBLANES):
            ks = slice(k * SUBLANES, (k + 1) * SUBLANES)
            end_k = a2[ks, :] * carry + b2[ks, :]
            ends.append(end_k)
            carry = jnp.broadcast_to(end_k[SUBLANES - 1:SUBLANES, :], (SUBLANES, CHUNK))
        hst_ref[:, cs] = carry
        h_in = jnp.where(is_row0, state0[0:1, :], pltpu.roll(jnp.concatenate(ends, axis=0), 1, 0))
        for r in range(n_blocks):
            h_r = hs[r] + ps[r] * h_in
            for s in range(n_slabs):
                p_out[s, pl.ds(r, g_rows, stride=n_blocks), :] = h_r[:, s * LANES:(s + 1) * LANES]
        h_time = jnp.concatenate([p_out[s] for s in range(n_slabs)], axis=1)
        o_ref[0, :, cs] = (h_time * _gelu_tanh(ug)).astype(BF16)


def _lru_branch(h, w_in, l, col0, conv_w, conv_b, wab, b_a, b_x, lam):
    bsz, s, d = h.shape
    width = conv_w.shape[1]
    t = ROW_TILE
    cx, cg = col0 // width, col0 // width + 1
    full = lambda shape: pl.BlockSpec(shape, lambda b, i: (0,) * len(shape))
    return pl.pallas_call(
        _lru_kernel,
        grid=(bsz, s // t),
        in_specs=[
            pl.BlockSpec((1, t, d), lambda b, i: (b, i, 0)),
            pl.BlockSpec((None, d, width), lambda b, i: (l, 0, cx)),
            pl.BlockSpec((None, d, width), lambda b, i: (l, 0, cg)),
            full(conv_w.shape), full((1, width)),
            pl.BlockSpec((None,) + wab.shape[1:], lambda b, i: (l, 0, 0, 0)),
            full((1, width)), full((1, width)), full((1, width)),
        ],
        out_specs=pl.BlockSpec((1, t, width), lambda b, i: (b, i, 0)),
        out_shape=jax.ShapeDtypeStruct((bsz, s, width), BF16),
        scratch_shapes=[
            pltpu.VMEM((SUBLANES, width), F32),
            pltpu.VMEM((SUBLANES, width), F32),
            pltpu.VMEM((2 * width // CHUNK, CHUNK // LANES, t, LANES), F32),
        ],
        compiler_params=_params(("arbitrary", "arbitrary")),
        name="lru_branch",
    )(h, w_in, w_in, conv_w, conv_b.reshape(1, width), wab,
      b_a.reshape(1, width), b_x.reshape(1, width), lam.reshape(1, width))


def _sc_kernel(h_ref, wb_ref, wc_ref, wx_ref, cw_ref, o_ref, tail_ref):
    i = pl.program_id(1)
    t = h_ref.shape[1]
    n_chunks = o_ref.shape[2] // CHUNK
    k_conv = cw_ref.shape[0]

    @pl.when(i == 0)
    def _():
        tail_ref[...] = jnp.zeros_like(tail_ref)

    h = h_ref[0]
    for j in range(n_chunks):
        cs = slice(j * CHUNK, (j + 1) * CHUNK)
        ub = jnp.dot(h, wb_ref[:, cs], preferred_element_type=F32)
        uc = jnp.dot(h, wc_ref[:, cs], preferred_element_type=F32)
        ux = jnp.dot(h, wx_ref[:, cs], preferred_element_type=F32)
        cx = uc * ux
        ext = jnp.concatenate([tail_ref[:, cs], cx], axis=0)
        tail_ref[:, cs] = cx[t - SUBLANES:, :]
        v = cw_ref[k_conv - 1:k_conv, cs] * cx
        for k in range(k_conv - 1):
            v = v + cw_ref[k:k + 1, cs] * _shift_rows(ext, k_conv - 1 - k)[SUBLANES:, :]
        o_ref[0, :, cs] = (ub * v).astype(BF16)


def _sc_branch(h, w_in, l, col0, conv_w):
    bsz, s, d = h.shape
    width = conv_w.shape[1]
    t = ROW_TILE
    c0 = col0 // width
    return pl.pallas_call(
        _sc_kernel,
        grid=(bsz, s // t),
        in_specs=[
            pl.BlockSpec((1, t, d), lambda b, i: (b, i, 0)),
            pl.BlockSpec((None, d, width), lambda b, i: (l, 0, c0)),
            pl.BlockSpec((None, d, width), lambda b, i: (l, 0, c0 + 1)),
            pl.BlockSpec((None, d, width), lambda b, i: (l, 0, c0 + 2)),
            pl.BlockSpec(conv_w.shape, lambda b, i: (0, 0)),
        ],
        out_specs=pl.BlockSpec((1, t, width), lambda b, i: (b, i, 0)),
        out_shape=jax.ShapeDtypeStruct((bsz, s, width), BF16),
        scratch_shapes=[pltpu.VMEM((SUBLANES, width), F32)],
        compiler_params=_params(("arbitrary", "arbitrary")),
        name="sc_branch",
    )(h, w_in, w_in, w_in, conv_w)


POOL_TAIL = 16


def _pool_kernel(h_ref, wp_ref, o_ref, tail_ref):
    i = pl.program_id(1)
    t = h_ref.shape[1]
    group_dim = o_ref.shape[2] // len(POOL_WINDOWS)

    @pl.when(i == 0)
    def _():
        tail_ref[...] = jnp.zeros_like(tail_ref)

    h = h_ref[0]
    pos = lax.broadcasted_iota(jnp.int32, (t, group_dim), 0) + i * t
    for g, w in enumerate(POOL_WINDOWS):
        cs = slice(g * group_dim, (g + 1) * group_dim)
        u = jnp.dot(h, wp_ref[:, cs], preferred_element_type=F32)
        acc = jnp.concatenate([tail_ref[:, cs], u], axis=0)
        tail_ref[:, cs] = u[t - POOL_TAIL:, :]
        d = 1
        while d < w:
            acc = acc + pltpu.roll(acc, d, 0)
            d *= 2
        cnt = jnp.minimum(pos + 1, w).astype(F32)
        o_ref[0, :, cs] = (acc[POOL_TAIL:, :] / cnt - u).astype(BF16)


def _pool_branch(h, w_in, l, col0, width):
    bsz, s, d = h.shape
    t = ROW_TILE
    c0 = col0 // width
    return pl.pallas_call(
        _pool_kernel,
        grid=(bsz, s // t),
        in_specs=[
            pl.BlockSpec((1, t, d), lambda b, i: (b, i, 0)),
            pl.BlockSpec((None, d, width), lambda b, i: (l, 0, c0)),
        ],
        out_specs=pl.BlockSpec((1, t, width), lambda b, i: (b, i, 0)),
        out_shape=jax.ShapeDtypeStruct((bsz, s, width), BF16),
        scratch_shapes=[pltpu.VMEM((POOL_TAIL, width), F32)],
        compiler_params=_params(("arbitrary", "arbitrary")),
        name="pool_branch",
    )(h, w_in)


def _merge_kernel(h_ref, ul_ref, us_ref, up_ref, wg0_ref, wg1_ref, wg2_ref,
                  wl_ref, ws_ref, wp_ref, ps_ref, o_ref):
    h = h_ref[...]
    dot = functools.partial(jnp.dot, preferred_element_type=F32)
    m = _sigmoid(dot(h, wg0_ref[...])) * dot(ul_ref[...], wl_ref[...])
    m = m + _sigmoid(dot(h, wg1_ref[...])) * dot(us_ref[...], ws_ref[...])
    m = m + _sigmoid(dot(h, wg2_ref[...])) * (dot(up_ref[...], wp_ref[0]) * ps_ref[...])
    o_ref[...] = m.astype(BF16)


def _merge(h, u_lru, u_sc, u_pool, w_in, l, gate_col0, w_lru_out, w_sc_out, pool_w, pool_scale):
    n, d = h.shape
    t, tn = MERGE_ROW_TILE, MERGE_COL_TILE
    _, n_groups, group_in, group_out = pool_w.shape
    assert group_out == tn
    g0 = gate_col0 // tn
    per_branch = d // tn
    rows = lambda width: pl.BlockSpec((t, width), lambda i, c: (i, 0))
    gate = lambda br: pl.BlockSpec((None, d, tn), lambda i, c: (l, 0, g0 + br * per_branch + c))
    return pl.pallas_call(
        _merge_kernel,
        grid=(n // t, d // tn),
        in_specs=[
            rows(d), rows(u_lru.shape[1]), rows(u_sc.shape[1]),
            pl.BlockSpec((t, group_in), lambda i, c: (i, c)),
            gate(0), gate(1), gate(2),
            pl.BlockSpec((None, w_lru_out.shape[1], tn), lambda i, c: (l, 0, c)),
            pl.BlockSpec((None, w_sc_out.shape[1], tn), lambda i, c: (l, 0, c)),
            pl.BlockSpec((None, 1, group_in, group_out), lambda i, c: (l, c, 0, 0)),
            pl.BlockSpec((1, tn), lambda i, c: (0, c)),
        ],
        out_specs=pl.BlockSpec((t, tn), lambda i, c: (i, c)),
        out_shape=jax.ShapeDtypeStruct((n, d), BF16),
        compiler_params=_params(("arbitrary", "arbitrary")),
        name="merge",
    )(h, u_lru, u_sc, u_pool, w_in, w_in, w_in, w_lru_out, w_sc_out, pool_w,
      pool_scale.reshape(1, d))


def _out_kernel(m_ref, w_ref, x_ref, g_ref, gain_ref, sc_ref, sh_ref, xo_ref, ho_ref):
    y = jnp.dot(m_ref[...], w_ref[...], preferred_element_type=F32)
    x_new = x_ref[...] + g_ref[0] * y
    xo_ref[...] = x_new
    ho_ref[...] = _mod_norm(x_new, gain_ref[...], sc_ref[0], sh_ref[0]).astype(BF16)


def _out_proj(merged, w_o, l, x, gate, gain, sc, sh, rows_per_batch):
    n, d = x.shape
    t = OUT_ROW_TILE
    tiles_per_batch = rows_per_batch // t
    rows = pl.BlockSpec((t, d), lambda i: (i, 0))
    vec = pl.BlockSpec((1, 1, d), lambda i: (i // tiles_per_batch, 0, 0))
    return pl.pallas_call(
        _out_kernel,
        grid=(n // t,),
        in_specs=[rows, pl.BlockSpec((None, d, d), lambda i: (l, 0, 0)), rows, vec,
                  pl.BlockSpec((1, d), lambda i: (0, 0)), vec, vec],
        out_specs=[rows, rows],
        out_shape=[jax.ShapeDtypeStruct((n, d), F32), jax.ShapeDtypeStruct((n, d), BF16)],
        compiler_params=_params(("arbitrary",)),
        name="out_proj",
    )(merged, w_o, x, gate, gain.reshape(1, d), sc, sh)


def _ffn_kernel(h_ref, w1_ref, w3_ref, w2_ref, x_ref, g_ref, gain_ref, *rest, final):
    if final:
        (xo_ref,) = rest
    else:
        sc_ref, sh_ref, xo_ref, ho_ref = rest
    acc_ref = xo_ref
    f = pl.program_id(1)

    @pl.when(f == 0)
    def _():
        acc_ref[...] = jnp.zeros_like(acc_ref)

    h = h_ref[...]
    a = jnp.dot(h, w1_ref[...], preferred_element_type=F32)
    b = jnp.dot(h, w3_ref[...], preferred_element_type=F32)
    act = ((a * _sigmoid(a)) * b).astype(BF16)
    acc_ref[...] += jnp.dot(act, w2_ref[...], preferred_element_type=F32)

    @pl.when(f == pl.num_programs(1) - 1)
    def _():
        x_new = x_ref[...] + g_ref[0] * acc_ref[...]
        if final:
            xo_ref[...] = x_new * lax.rsqrt(
                jnp.mean(x_new * x_new, axis=-1, keepdims=True) + EPS) * gain_ref[...]
        else:
            xo_ref[...] = x_new
            ho_ref[...] = _mod_norm(x_new, gain_ref[...], sc_ref[0], sh_ref[0]).astype(BF16)


def _ffn(h, w1, w3, w2, l, x, gate, gain, sc, sh, rows_per_batch, final):
    n, d = x.shape
    d_ff = w1.shape[2]
    t, tf = FFN_ROW_TILE, FFN_COL_TILE
    tiles_per_batch = rows_per_batch // t
    rows = pl.BlockSpec((t, d), lambda i, f: (i, 0))
    rows_once = pl.BlockSpec((t, d), lambda i, f: (i, 0), pipeline_mode=pl.Buffered(1))
    vec = pl.BlockSpec((1, 1, d), lambda i, f: (i // tiles_per_batch, 0, 0))
    in_specs = [rows,
                pl.BlockSpec((None, d, tf), lambda i, f: (l, 0, f)),
                pl.BlockSpec((None, d, tf), lambda i, f: (l, 0, f)),
                pl.BlockSpec((None, tf, d), lambda i, f: (l, f, 0)),
                rows_once, vec, pl.BlockSpec((1, d), lambda i, f: (0, 0))]
    args = [h, w1, w3, w2, x, gate, gain.reshape(1, d)]
    if final:
        out_specs = rows_once
        out_shape = jax.ShapeDtypeStruct((n, d), F32)
        scratch = []
    else:
        in_specs += [vec, vec]
        args += [sc, sh]
        out_specs = [rows_once, rows_once]
        out_shape = [jax.ShapeDtypeStruct((n, d), F32), jax.ShapeDtypeStruct((n, d), BF16)]
        scratch = []
    return pl.pallas_call(
        functools.partial(_ffn_kernel, final=final),
        grid=(n // t, d_ff // tf),
        in_specs=in_specs,
        out_specs=out_specs,
        out_shape=out_shape,
        scratch_shapes=scratch,
        compiler_params=_params(("arbitrary", "arbitrary")),
        name="ffn_final" if final else "ffn",
    )(*args)


def _block_diag_heads(w):
    depth, heads, hd, _ = w.shape
    per = CHUNK // hd
    w5 = w.reshape(depth, heads // per, per, hd, hd)
    eye = jnp.eye(per, dtype=w.dtype)
    bd = jnp.einsum("pq,ljpik->ljpiqk", eye, w5)
    return bd.reshape(depth, heads // per, CHUNK, CHUNK)


def kernel(x, c, w_mod, b_mod, norm_mix, w_in, lru_conv_w, lru_conv_b, lru_w_a, lru_b_a, lru_w_x, lru_b_x, lru_lambda, lru_w_out, sc_conv_w, sc_w_out, pool_w, pool_scale, w_o, norm_ffn, ffn_w1, ffn_w3, ffn_w2, norm_final):
    bsz, s, d = x.shape
    depth = w_in.shape[0]
    lru_width = lru_conv_w.shape[2]
    sc_width = sc_conv_w.shape[2]
    pool_width = pool_w.shape[1] * pool_w.shape[2]
    lru_col0 = 0
    sc_col0 = 2 * lru_width
    pool_col0 = sc_col0 + 3 * sc_width
    gate_col0 = pool_col0 + pool_width

    mod = _modulation(c, w_mod, b_mod)
    w_in_b = w_in.astype(BF16)
    wab = jnp.concatenate([_block_diag_heads(lru_w_a), _block_diag_heads(lru_w_x)],
                          axis=-1).astype(BF16)
    lru_w_out_b = lru_w_out.astype(BF16)
    sc_w_out_b = sc_w_out.astype(BF16)
    pool_w_b = pool_w.astype(BF16)
    w_o_b = w_o.astype(BF16)
    w1_b, w3_b, w2_b = ffn_w1.astype(BF16), ffn_w3.astype(BF16), ffn_w2.astype(BF16)

    sh1, sc1, g1, sh2, sc2, g2 = (mod[0, k] for k in range(N_MOD))
    h = _first_norm(x, norm_mix[0], sc1, sh1)
    xf = x.reshape(bsz * s, d)
    for l in range(depth):
        sh1, sc1, g1, sh2, sc2, g2 = (mod[l, k] for k in range(N_MOD))
        u_lru = _lru_branch(h, w_in_b, l, lru_col0, lru_conv_w[l], lru_conv_b[l], wab,
                            lru_b_a[l], lru_b_x[l], lru_lambda[l])
        u_sc = _sc_branch(h, w_in_b, l, sc_col0, sc_conv_w[l])
        u_pool = _pool_branch(h, w_in_b, l, pool_col0, pool_width)
        flat = lambda a: a.reshape(bsz * s, a.shape[-1])
        merged = _merge(flat(h), flat(u_lru), flat(u_sc), flat(u_pool), w_in_b, l, gate_col0,
                        lru_w_out_b, sc_w_out_b, pool_w_b, pool_scale[l])
        xf, h2 = _out_proj(merged, w_o_b, l, xf, g1, norm_ffn[l], sc2, sh2, s)
        if l + 1 < depth:
            nsh1, nsc1 = mod[l + 1, 0], mod[l + 1, 1]
            xf, hn = _ffn(h2, w1_b, w3_b, w2_b, l, xf, g2, norm_mix[l + 1], nsc1, nsh1,
                          s, final=False)
            h = hn.reshape(bsz, s, d)
        else:
            xf = _ffn(h2, w1_b, w3_b, w2_b, l, xf, g2, norm_final, None, None, s,
                      final=True)
    return xf.reshape(bsz, s, d)
```

```python
import functools

import jax
import jax.numpy as jnp
from jax import lax
from jax.experimental import pallas as pl
from jax.experimental.pallas import tpu as pltpu

F32 = jnp.float32
BF16 = jnp.bfloat16

EPS = 1e-6
LRU_C = 8.0
LRU_HEAD_DIM = 64
POOL_WINDOWS = (2, 4, 8, 16)
N_MOD = 6

SUBLANES = 8
LANES = 128
CHUNK = 256
ROW_TILE = 512
NORM_ROW_TILE = 1024
MERGE_ROW_TILE = 1024
MERGE_COL_TILE = 512
OUT_ROW_TILE = 512
FFN_ROW_TILE = 512
FFN_COL_TILE = 512
MOD_COL_TILE = 1024
VMEM_LIMIT = 56 * 1024 * 1024


def _params(sem):
    return pltpu.CompilerParams(dimension_semantics=sem, vmem_limit_bytes=VMEM_LIMIT)


def _sigmoid(x):
    return 0.5 * jnp.tanh(0.5 * x) + 0.5


def _gelu_tanh(x):
    return 0.5 * x * (1.0 + jnp.tanh(0.7978845608028654 * (x + 0.044715 * (x * x * x))))


def _mod_norm(x, gain, scale, shift):
    y = x * lax.rsqrt(jnp.mean(x * x, axis=-1, keepdims=True) + EPS)
    return (y * gain) * (1.0 + scale) + shift


def _shift_rows(ext, s):
    if s == 0:
        return ext
    return pltpu.roll(ext, s, 0)


def _mod_kernel(c_ref, w_ref, b_ref, o_ref):
    c = c_ref[...]
    s = (c * _sigmoid(c)).astype(BF16)
    o_ref[0] = jnp.dot(s, w_ref[0].astype(BF16), preferred_element_type=F32) + b_ref[0]


def _modulation(c, w_mod, b_mod):
    depth, d, width = w_mod.shape
    bsz = c.shape[0]
    c_pad = jnp.zeros((SUBLANES, d), F32).at[:bsz].set(c)
    out = pl.pallas_call(
        _mod_kernel,
        grid=(depth, width // MOD_COL_TILE),
        in_specs=[
            pl.BlockSpec((SUBLANES, d), lambda l, n: (0, 0)),
            pl.BlockSpec((1, d, MOD_COL_TILE), lambda l, n: (l, 0, n)),
            pl.BlockSpec((1, 1, MOD_COL_TILE), lambda l, n: (l, 0, n)),
        ],
        out_specs=pl.BlockSpec((1, SUBLANES, MOD_COL_TILE), lambda l, n: (l, 0, n)),
        out_shape=jax.ShapeDtypeStruct((depth, SUBLANES, width), F32),
        compiler_params=_params(("arbitrary", "arbitrary")),
        name="modulation",
    )(c_pad, w_mod, b_mod.reshape(depth, 1, width))
    return out[:, :bsz].reshape(depth, bsz, N_MOD, 1, d).transpose(0, 2, 1, 3, 4)


def _norm_kernel(x_ref, gain_ref, sc_ref, sh_ref, o_ref):
    o_ref[0] = _mod_norm(x_ref[0], gain_ref[...], sc_ref[0], sh_ref[0]).astype(BF16)


def _first_norm(x, gain, sc, sh):
    bsz, s, d = x.shape
    t = NORM_ROW_TILE
    vec = pl.BlockSpec((1, 1, d), lambda b, i: (b, 0, 0))
    return pl.pallas_call(
        _norm_kernel,
        grid=(bsz, s // t),
        in_specs=[
            pl.BlockSpec((1, t, d), lambda b, i: (b, i, 0)),
            pl.BlockSpec((1, d), lambda b, i: (0, 0)),
            vec, vec,
        ],
        out_specs=pl.BlockSpec((1, t, d), lambda b, i: (b, i, 0)),
        out_shape=jax.ShapeDtypeStruct((bsz, s, d), BF16),
        compiler_params=_params(("arbitrary", "arbitrary")),
        name="first_norm",
    )(x, gain.reshape(1, d), sc, sh)


def _group_scan(a, b):
    row = lax.broadcasted_iota(jnp.int32, a.shape, 0) % SUBLANES
    for d in (1, 2, 4):
        keep = row >= d
        a_prev = pltpu.roll(a, d, 0)
        b_prev = pltpu.roll(b, d, 0)
        b = jnp.where(keep, a * b_prev + b, b)
        a = jnp.where(keep, a * a_prev, a)
    return a, b


def _lru_kernel(h_ref, wx_ref, wg_ref, cw_ref, cb_ref, wab_ref, ba_ref, bx_ref, lam_ref,
                o_ref, tail_ref, hst_ref, perm_ref):
    i = pl.program_id(1)
    t = h_ref.shape[1]
    n_blocks = SUBLANES
    g_rows = t // n_blocks
    n_chunks = o_ref.shape[2] // CHUNK
    n_slabs = CHUNK // LANES
    k_conv = cw_ref.shape[0]

    h = h_ref[0]
    row = lax.broadcasted_iota(jnp.int32, (g_rows, CHUNK), 0)
    is_row0 = row == 0
    seq_start = (row + i * g_rows) == 0

    def project(j):
        cols = slice(j * CHUNK, (j + 1) * CHUNK)
        return (jnp.dot(h, wx_ref[:, cols], preferred_element_type=F32),
                jnp.dot(h, wg_ref[:, cols], preferred_element_type=F32))

    projected = project(0)
    for j in range(n_chunks):
        cs = slice(j * CHUNK, (j + 1) * CHUNK)
        ux, ug = projected
        if j + 1 < n_chunks:
            projected = project(j + 1)
        prev_tail = tail_ref[:, cs]
        tail_ref[:, cs] = ux[t - SUBLANES:, :]
        p_in, p_out = perm_ref.at[2 * j], perm_ref.at[2 * j + 1]
        for s in range(n_slabs):
            p_in[s] = ux[:, s * LANES:(s + 1) * LANES]
        x = [jnp.concatenate([p_in[s, pl.ds(r, g_rows, stride=n_blocks), :]
                              for s in range(n_slabs)], axis=1) for r in range(n_blocks)]
        late = {q: jnp.where(is_row0, prev_tail[q:q + 1, :], pltpu.roll(x[q], 1, 0))
                for q in range(n_blocks - (k_conv - 1), n_blocks)}
        xc = []
        for r in range(n_blocks):
            acc = cb_ref[:, cs]
            for k in range(k_conv):
                s = k_conv - 1 - k
                src = x[r - s] if r >= s else late[r - s + n_blocks]
                acc = acc + cw_ref[k:k + 1, cs] * src
            xc.append(acc)
        ri = jnp.dot(jnp.concatenate(xc, axis=0).astype(BF16), wab_ref[j],
                     preferred_element_type=F32)
        z = -lam_ref[:, cs]
        neg_c_softplus = -LRU_C * (jnp.maximum(z, 0.0) + jnp.log1p(jnp.exp(-jnp.abs(z))))
        hs, ps = [], []
        for r in range(n_blocks):
            rs = slice(r * g_rows, (r + 1) * g_rows)
            gate_r = _sigmoid(ri[rs, :CHUNK] + ba_ref[:, cs])
            gate_i = _sigmoid(ri[rs, CHUNK:] + bx_ref[:, cs])
            log_a = neg_c_softplus * gate_r
            a = jnp.exp(log_a)
            th = jnp.tanh(log_a)
            mult = jnp.sqrt(-2.0 * th / (1.0 - th))
            if r == 0:
                mult = jnp.where(seq_start, 1.0, mult)
            b = mult * (gate_i * xc[r])
            hs.append(b if r == 0 else a * hs[-1] + b)
            ps.append(a if r == 0 else a * ps[-1])
        a2, b2 = _group_scan(ps[-1], hs[-1])
        state0 = hst_ref[:, cs]
        carry = state0
        ends = []
        for k in range(g_rows // SUBLANES):
            ks = slice(k * SUBLANES, (k + 1) * SUBLANES)
            end_k = a2[ks, :] * carry + b2[ks, :]
            ends.append(end_k)
            carry = jnp.broadcast_to(end_k[SUBLANES - 1:SUBLANES, :], (SUBLANES, CHUNK))
        hst_ref[:, cs] = carry
        h_in = jnp.where(is_row0, state0[0:1, :], pltpu.roll(jnp.concatenate(ends, axis=0), 1, 0))
        for r in range(n_blocks):
            h_r = hs[r] + ps[r] * h_in
            for s in range(n_slabs):
                p_out[s, pl.ds(r, g_rows, stride=n_blocks), :] = h_r[:, s * LANES:(s + 1) * LANES]
        h_time = jnp.concatenate([p_out[s] for s in range(n_slabs)], axis=1)
        o_ref[0, :, cs] = (h_time * _gelu_tanh(ug)).astype(BF16)


def _sc_kernel(h_ref, wb_ref, wc_ref, wx_ref, cw_ref, o_ref, tail_ref):
    i = pl.program_id(1)
    t = h_ref.shape[1]
    n_chunks = o_ref.shape[2] // CHUNK
    k_conv = cw_ref.shape[0]

    h = h_ref[0]
    for j in range(n_chunks):
        cs = slice(j * CHUNK, (j + 1) * CHUNK)
        ub = jnp.dot(h, wb_ref[:, cs], preferred_element_type=F32)
        uc = jnp.dot(h, wc_ref[:, cs], preferred_element_type=F32)
        ux = jnp.dot(h, wx_ref[:, cs], preferred_element_type=F32)
        cx = uc * ux
        ext = jnp.concatenate([tail_ref[:, cs], cx], axis=0)
        tail_ref[:, cs] = cx[t - SUBLANES:, :]
        v = cw_ref[k_conv - 1:k_conv, cs] * cx
        for k in range(k_conv - 1):
            v = v + cw_ref[k:k + 1, cs] * _shift_rows(ext, k_conv - 1 - k)[SUBLANES:, :]
        o_ref[0, :, cs] = (ub * v).astype(BF16)


POOL_TAIL = 16


def _pool_kernel(h_ref, wp_ref, o_ref, tail_ref):
    i = pl.program_id(1)
    t = h_ref.shape[1]
    group_dim = o_ref.shape[2] // len(POOL_WINDOWS)

    h = h_ref[0]
    pos = lax.broadcasted_iota(jnp.int32, (t, group_dim), 0) + i * t
    for g, w in enumerate(POOL_WINDOWS):
        cs = slice(g * group_dim, (g + 1) * group_dim)
        u = jnp.dot(h, wp_ref[:, cs], preferred_element_type=F32)
        acc = jnp.concatenate([tail_ref[:, cs], u], axis=0)
        tail_ref[:, cs] = u[t - POOL_TAIL:, :]
        d = 1
        while d < w:
            acc = acc + pltpu.roll(acc, d, 0)
            d *= 2
        cnt = jnp.minimum(pos + 1, w).astype(F32)
        o_ref[0, :, cs] = (acc[POOL_TAIL:, :] / cnt - u).astype(BF16)


def _mix_kernel(h_ref, wlx_ref, wlg_ref, wsb_ref, wsc_ref, wsx_ref, wp_ref,
                lcw_ref, lcb_ref, wab_ref, ba_ref, bx_ref, lam_ref, scw_ref,
                ol_ref, os_ref, op_ref,
                ltail_ref, hst_ref, perm_ref, stail_ref, ptail_ref):
    @pl.when(pl.program_id(1) == 0)
    def _():
        for ref in (ltail_ref, hst_ref, stail_ref, ptail_ref):
            ref[...] = jnp.zeros_like(ref)

    _lru_kernel(h_ref, wlx_ref, wlg_ref, lcw_ref, lcb_ref, wab_ref, ba_ref, bx_ref, lam_ref,
                ol_ref, ltail_ref, hst_ref, perm_ref)
    _sc_kernel(h_ref, wsb_ref, wsc_ref, wsx_ref, scw_ref, os_ref, stail_ref)
    _pool_kernel(h_ref, wp_ref, op_ref, ptail_ref)


def _mix_branches(h, w_in, l, lru_conv_w, lru_conv_b, wab, b_a, b_x, lam, sc_conv_w):
    bsz, s, d = h.shape
    width = lru_conv_w.shape[1]
    t = ROW_TILE
    w_view = lambda c: pl.BlockSpec((None, d, width), lambda b, i: (l, 0, c),
                                    pipeline_mode=pl.Buffered(1))
    full = lambda shape: pl.BlockSpec(shape, lambda b, i: (0,) * len(shape))
    out = pl.BlockSpec((1, t, width), lambda b, i: (b, i, 0))
    vec = lambda a: a.reshape(1, width)
    return pl.pallas_call(
        _mix_kernel,
        grid=(bsz, s // t),
        in_specs=[pl.BlockSpec((1, t, d), lambda b, i: (b, i, 0))]
        + [w_view(c) for c in range(6)]
        + [full(lru_conv_w.shape), full((1, width)),
           pl.BlockSpec((None,) + wab.shape[1:], lambda b, i: (l, 0, 0, 0)),
           full((1, width)), full((1, width)), full((1, width)), full(sc_conv_w.shape)],
        out_specs=[out, out, out],
        out_shape=[jax.ShapeDtypeStruct((bsz, s, width), BF16)] * 3,
        scratch_shapes=[
            pltpu.VMEM((SUBLANES, width), F32),
            pltpu.VMEM((SUBLANES, width), F32),
            pltpu.VMEM((2 * width // CHUNK, CHUNK // LANES, t, LANES), F32),
            pltpu.VMEM((SUBLANES, width), F32),
            pltpu.VMEM((POOL_TAIL, width), F32),
        ],
        compiler_params=_params(("arbitrary", "arbitrary")),
        name="mix_branches",
    )(h, *([w_in] * 6), lru_conv_w, vec(lru_conv_b), wab, vec(b_a), vec(b_x), vec(lam),
      sc_conv_w)


def _merge_kernel(h_ref, ul_ref, us_ref, up_ref, wg0_ref, wg1_ref, wg2_ref,
                  wl_ref, ws_ref, wp_ref, ps_ref, o_ref):
    h = h_ref[...]
    dot = functools.partial(jnp.dot, preferred_element_type=F32)
    m = _sigmoid(dot(h, wg0_ref[...])) * dot(ul_ref[...], wl_ref[...])
    m = m + _sigmoid(dot(h, wg1_ref[...])) * dot(us_ref[...], ws_ref[...])
    m = m + _sigmoid(dot(h, wg2_ref[...])) * (dot(up_ref[...], wp_ref[0]) * ps_ref[...])
    o_ref[...] = m.astype(BF16)


def _merge(h, u_lru, u_sc, u_pool, w_in, l, gate_col0, w_lru_out, w_sc_out, pool_w, pool_scale):
    n, d = h.shape
    t, tn = MERGE_ROW_TILE, MERGE_COL_TILE
    _, n_groups, group_in, group_out = pool_w.shape
    assert group_out == tn
    g0 = gate_col0 // tn
    per_branch = d // tn
    rows = lambda width: pl.BlockSpec((t, width), lambda i, c: (i, 0))
    gate = lambda br: pl.BlockSpec((None, d, tn), lambda i, c: (l, 0, g0 + br * per_branch + c))
    return pl.pallas_call(
        _merge_kernel,
        grid=(n // t, d // tn),
        in_specs=[
            rows(d), rows(u_lru.shape[1]), rows(u_sc.shape[1]),
            pl.BlockSpec((t, group_in), lambda i, c: (i, c)),
            gate(0), gate(1), gate(2),
            pl.BlockSpec((None, w_lru_out.shape[1], tn), lambda i, c: (l, 0, c)),
            pl.BlockSpec((None, w_sc_out.shape[1], tn), lambda i, c: (l, 0, c)),
            pl.BlockSpec((None, 1, group_in, group_out), lambda i, c: (l, c, 0, 0)),
            pl.BlockSpec((1, tn), lambda i, c: (0, c)),
        ],
        out_specs=pl.BlockSpec((t, tn), lambda i, c: (i, c)),
        out_shape=jax.ShapeDtypeStruct((n, d), BF16),
        compiler_params=_params(("arbitrary", "arbitrary")),
        name="merge",
    )(h, u_lru, u_sc, u_pool, w_in, w_in, w_in, w_lru_out, w_sc_out, pool_w,
      pool_scale.reshape(1, d))


def _out_kernel(m_ref, w_ref, x_ref, g_ref, gain_ref, sc_ref, sh_ref, xo_ref, ho_ref):
    y = jnp.dot(m_ref[...], w_ref[...], preferred_element_type=F32)
    x_new = x_ref[...] + g_ref[0] * y
    xo_ref[...] = x_new
    ho_ref[...] = _mod_norm(x_new, gain_ref[...], sc_ref[0], sh_ref[0]).astype(BF16)


def _out_proj(merged, w_o, l, x, gate, gain, sc, sh, rows_per_batch):
    n, d = x.shape
    t = OUT_ROW_TILE
    tiles_per_batch = rows_per_batch // t
    rows = pl.BlockSpec((t, d), lambda i: (i, 0))
    vec = pl.BlockSpec((1, 1, d), lambda i: (i // tiles_per_batch, 0, 0))
    return pl.pallas_call(
        _out_kernel,
        grid=(n // t,),
        in_specs=[rows, pl.BlockSpec((None, d, d), lambda i: (l, 0, 0)), rows, vec,
                  pl.BlockSpec((1, d), lambda i: (0, 0)), vec, vec],
        out_specs=[rows, rows],
        out_shape=[jax.ShapeDtypeStruct((n, d), F32), jax.ShapeDtypeStruct((n, d), BF16)],
        compiler_params=_params(("arbitrary",)),
        name="out_proj",
    )(merged, w_o, x, gate, gain.reshape(1, d), sc, sh)


def _ffn_kernel(h_ref, w1_ref, w3_ref, w2_ref, x_ref, g_ref, gain_ref, *rest, final):
    if final:
        (xo_ref,) = rest
    else:
        sc_ref, sh_ref, xo_ref, ho_ref = rest
    acc_ref = xo_ref
    f = pl.program_id(1)

    @pl.when(f == 0)
    def _():
        acc_ref[...] = jnp.zeros_like(acc_ref)

    h = h_ref[...]
    a = jnp.dot(h, w1_ref[...], preferred_element_type=F32)
    b = jnp.dot(h, w3_ref[...], preferred_element_type=F32)
    act = ((a * _sigmoid(a)) * b).astype(BF16)
    acc_ref[...] += jnp.dot(act, w2_ref[...], preferred_element_type=F32)

    @pl.when(f == pl.num_programs(1) - 1)
    def _():
        x_new = x_ref[...] + g_ref[0] * acc_ref[...]
        if final:
            xo_ref[...] = x_new * lax.rsqrt(
                jnp.mean(x_new * x_new, axis=-1, keepdims=True) + EPS) * gain_ref[...]
        else:
            xo_ref[...] = x_new
            ho_ref[...] = _mod_norm(x_new, gain_ref[...], sc_ref[0], sh_ref[0]).astype(BF16)


def _ffn(h, w1, w3, w2, l, x, gate, gain, sc, sh, rows_per_batch, final):
    n, d = x.shape
    d_ff = w1.shape[2]
    t, tf = FFN_ROW_TILE, FFN_COL_TILE
    tiles_per_batch = rows_per_batch // t
    rows = pl.BlockSpec((t, d), lambda i, f: (i, 0))
    vec = pl.BlockSpec((1, 1, d), lambda i, f: (i // tiles_per_batch, 0, 0))
    in_specs = [rows,
                pl.BlockSpec((None, d, tf), lambda i, f: (l, 0, f)),
                pl.BlockSpec((None, d, tf), lambda i, f: (l, 0, f)),
                pl.BlockSpec((None, tf, d), lambda i, f: (l, f, 0)),
                rows, vec, pl.BlockSpec((1, d), lambda i, f: (0, 0))]
    args = [h, w1, w3, w2, x, gate, gain.reshape(1, d)]
    if final:
        out_specs = rows
        out_shape = jax.ShapeDtypeStruct((n, d), F32)
        scratch = []
    else:
        in_specs += [vec, vec]
        args += [sc, sh]
        out_specs = [rows, rows]
        out_shape = [jax.ShapeDtypeStruct((n, d), F32), jax.ShapeDtypeStruct((n, d), BF16)]
        scratch = []
    return pl.pallas_call(
        functools.partial(_ffn_kernel, final=final),
        grid=(n // t, d_ff // tf),
        in_specs=in_specs,
        out_specs=out_specs,
        out_shape=out_shape,
        scratch_shapes=scratch,
        compiler_params=_params(("arbitrary", "arbitrary")),
        name="ffn_final" if final else "ffn",
    )(*args)


def _block_diag_heads(w):
    depth, heads, hd, _ = w.shape
    per = CHUNK // hd
    w5 = w.reshape(depth, heads // per, per, hd, hd)
    eye = jnp.eye(per, dtype=w.dtype)
    bd = jnp.einsum("pq,ljpik->ljpiqk", eye, w5)
    return bd.reshape(depth, heads // per, CHUNK, CHUNK)


def kernel(x, c, w_mod, b_mod, norm_mix, w_in, lru_conv_w, lru_conv_b, lru_w_a, lru_b_a, lru_w_x, lru_b_x, lru_lambda, lru_w_out, sc_conv_w, sc_w_out, pool_w, pool_scale, w_o, norm_ffn, ffn_w1, ffn_w3, ffn_w2, norm_final):
    bsz, s, d = x.shape
    depth = w_in.shape[0]
    lru_width = lru_conv_w.shape[2]
    sc_width = sc_conv_w.shape[2]
    pool_width = pool_w.shape[1] * pool_w.shape[2]
    assert lru_width == sc_width == pool_width
    gate_col0 = 2 * lru_width + 3 * sc_width + pool_width

    mod = _modulation(c, w_mod, b_mod)
    w_in_b = w_in.astype(BF16)
    wab = jnp.concatenate([_block_diag_heads(lru_w_a), _block_diag_heads(lru_w_x)],
                          axis=-1).astype(BF16)
    lru_w_out_b = lru_w_out.astype(BF16)
    sc_w_out_b = sc_w_out.astype(BF16)
    pool_w_b = pool_w.astype(BF16)
    w_o_b = w_o.astype(BF16)
    w1_b, w3_b, w2_b = ffn_w1.astype(BF16), ffn_w3.astype(BF16), ffn_w2.astype(BF16)

    sh1, sc1, g1, sh2, sc2, g2 = (mod[0, k] for k in range(N_MOD))
    h = _first_norm(x, norm_mix[0], sc1, sh1)
    xf = x.reshape(bsz * s, d)
    for l in range(depth):
        sh1, sc1, g1, sh2, sc2, g2 = (mod[l, k] for k in range(N_MOD))
        u_lru, u_sc, u_pool = _mix_branches(h, w_in_b, l, lru_conv_w[l], lru_conv_b[l], wab,
                                            lru_b_a[l], lru_b_x[l], lru_lambda[l], sc_conv_w[l])
        flat = lambda a: a.reshape(bsz * s, a.shape[-1])
        merged = _merge(flat(h), flat(u_lru), flat(u_sc), flat(u_pool), w_in_b, l, gate_col0,
                        lru_w_out_b, sc_w_out_b, pool_w_b, pool_scale[l])
        xf, h2 = _out_proj(merged, w_o_b, l, xf, g1, norm_ffn[l], sc2, sh2, s)
        if l + 1 < depth:
            nsh1, nsc1 = mod[l + 1, 0], mod[l + 1, 1]
            xf, hn = _ffn(h2, w1_b, w3_b, w2_b, l, xf, g2, norm_mix[l + 1], nsc1, nsh1,
                          s, final=False)
            h = hn.reshape(bsz, s, d)
        else:
            xf = _ffn(h2, w1_b, w3_b, w2_b, l, xf, g2, norm_final, None, None, s,
                      final=True)
    return xf.reshape(bsz, s, d)
```

```python
import functools

import jax
import jax.numpy as jnp
from jax import lax
from jax.experimental import pallas as pl
from jax.experimental.pallas import tpu as pltpu

F32 = jnp.float32
BF16 = jnp.bfloat16

EPS = 1e-6
LRU_C = 8.0
POOL_WINDOWS = (2, 4, 8, 16)
N_MOD = 6

SUBLANES = 8
LANES = 128
CHUNK = 256
ROW_TILE = 512
NORM_ROW_TILE = 1024
MERGE_ROW_TILE = 1024
MERGE_COL_TILE = 512
OUT_ROW_TILE = 512
FFN_ROW_TILE = 512
FFN_COL_TILE = 512
MOD_COL_TILE = 1024
VMEM_LIMIT = 56 * 1024 * 1024


def _params(sem):
    return pltpu.CompilerParams(dimension_semantics=sem, vmem_limit_bytes=VMEM_LIMIT)


def _sigmoid(x):
    return 0.5 * jnp.tanh(0.5 * x) + 0.5


def _gelu_tanh(x):
    return 0.5 * x * (1.0 + jnp.tanh(0.7978845608028654 * (x + 0.044715 * (x * x * x))))


def _mod_norm(x, gain, scale, shift):
    y = x * lax.rsqrt(jnp.mean(x * x, axis=-1, keepdims=True) + EPS)
    return y * (gain * (1.0 + scale)) + shift


def _shift_rows(ext, s):
    if s == 0:
        return ext
    return pltpu.roll(ext, s, 0)


def _mod_kernel(c_ref, w_ref, b_ref, o_ref):
    c = c_ref[...]
    s = (c * _sigmoid(c)).astype(BF16)
    o_ref[0] = jnp.dot(s, w_ref[0].astype(BF16), preferred_element_type=F32) + b_ref[0]


def _modulation(c, w_mod, b_mod):
    depth, d, width = w_mod.shape
    bsz = c.shape[0]
    c_pad = jnp.zeros((SUBLANES, d), F32).at[:bsz].set(c)
    out = pl.pallas_call(
        _mod_kernel,
        grid=(depth, width // MOD_COL_TILE),
        in_specs=[
            pl.BlockSpec((SUBLANES, d), lambda l, n: (0, 0)),
            pl.BlockSpec((1, d, MOD_COL_TILE), lambda l, n: (l, 0, n)),
            pl.BlockSpec((1, 1, MOD_COL_TILE), lambda l, n: (l, 0, n)),
        ],
        out_specs=pl.BlockSpec((1, SUBLANES, MOD_COL_TILE), lambda l, n: (l, 0, n)),
        out_shape=jax.ShapeDtypeStruct((depth, SUBLANES, width), F32),
        compiler_params=_params(("arbitrary", "arbitrary")),
        name="modulation",
    )(c_pad, w_mod, b_mod.reshape(depth, 1, width))
    return out[:, :bsz].reshape(depth, bsz, N_MOD, 1, d).transpose(0, 2, 1, 3, 4)


def _norm_kernel(x_ref, gain_ref, sc_ref, sh_ref, o_ref):
    o_ref[0] = _mod_norm(x_ref[0], gain_ref[...], sc_ref[0], sh_ref[0]).astype(BF16)


def _first_norm(x, gain, sc, sh):
    bsz, s, d = x.shape
    t = NORM_ROW_TILE
    vec = pl.BlockSpec((1, 1, d), lambda b, i: (b, 0, 0))
    return pl.pallas_call(
        _norm_kernel,
        grid=(bsz, s // t),
        in_specs=[
            pl.BlockSpec((1, t, d), lambda b, i: (b, i, 0)),
            pl.BlockSpec((1, d), lambda b, i: (0, 0)),
            vec, vec,
        ],
        out_specs=pl.BlockSpec((1, t, d), lambda b, i: (b, i, 0)),
        out_shape=jax.ShapeDtypeStruct((bsz, s, d), BF16),
        compiler_params=_params(("arbitrary", "arbitrary")),
        name="first_norm",
    )(x, gain.reshape(1, d), sc, sh)


def _group_scan(a, b):
    row = lax.broadcasted_iota(jnp.int32, a.shape, 0) % SUBLANES
    for d in (1, 2, 4):
        keep = row >= d
        a_prev = pltpu.roll(a, d, 0)
        b_prev = pltpu.roll(b, d, 0)
        b = jnp.where(keep, a * b_prev + b, b)
        a = jnp.where(keep, a * a_prev, a)
    return a, b


def _lru_kernel(h_ref, wx_ref, wg_ref, cw_ref, cb_ref, wab_ref, ba_ref, bx_ref, lam_ref,
                o_ref, tail_ref, hst_ref, perm_ref):
    i = pl.program_id(1)
    t = h_ref.shape[1]
    n_blocks = SUBLANES
    g_rows = t // n_blocks
    n_chunks = o_ref.shape[2] // CHUNK
    n_slabs = CHUNK // LANES
    k_conv = cw_ref.shape[0]

    h = h_ref[0]
    row = lax.broadcasted_iota(jnp.int32, (g_rows, CHUNK), 0)
    is_row0 = row == 0
    seq_start = (row + i * g_rows) == 0

    def project(j):
        cols = slice(j * CHUNK, (j + 1) * CHUNK)
        return (jnp.dot(h, wx_ref[:, cols], preferred_element_type=F32),
                jnp.dot(h, wg_ref[:, cols], preferred_element_type=F32))

    projected = project(0)
    for j in range(n_chunks):
        cs = slice(j * CHUNK, (j + 1) * CHUNK)
        ux, ug = projected
        if j + 1 < n_chunks:
            projected = project(j + 1)
        prev_tail = tail_ref[:, cs]
        tail_ref[:, cs] = ux[t - SUBLANES:, :]
        p_in, p_out = perm_ref.at[2 * j], perm_ref.at[2 * j + 1]
        for s in range(n_slabs):
            p_in[s] = ux[:, s * LANES:(s + 1) * LANES]
        x = [jnp.concatenate([p_in[s, pl.ds(r, g_rows, stride=n_blocks), :]
                              for s in range(n_slabs)], axis=1) for r in range(n_blocks)]
        late = {q: jnp.where(is_row0, prev_tail[q:q + 1, :], pltpu.roll(x[q], 1, 0))
                for q in range(n_blocks - (k_conv - 1), n_blocks)}
        xc = []
        for r in range(n_blocks):
            acc = cb_ref[:, cs]
            for k in range(k_conv):
                s = k_conv - 1 - k
                src = x[r - s] if r >= s else late[r - s + n_blocks]
                acc = acc + cw_ref[k:k + 1, cs] * src
            xc.append(acc)
        ri = jnp.dot(jnp.concatenate(xc, axis=0).astype(BF16), wab_ref[j],
                     preferred_element_type=F32)
        z = -lam_ref[:, cs]
        neg_c_softplus = -LRU_C * (jnp.maximum(z, 0.0) + jnp.log1p(jnp.exp(-jnp.abs(z))))
        hs, ps = [], []
        for r in range(n_blocks):
            rs = slice(r * g_rows, (r + 1) * g_rows)
            gate_r = _sigmoid(ri[rs, :CHUNK] + ba_ref[:, cs])
            gate_i = _sigmoid(ri[rs, CHUNK:] + bx_ref[:, cs])
            log_a = neg_c_softplus * gate_r
            a = jnp.exp(log_a)
            th = jnp.tanh(log_a)
            mult = jnp.sqrt(-2.0 * th / (1.0 - th))
            if r == 0:
                mult = jnp.where(seq_start, 1.0, mult)
            b = mult * (gate_i * xc[r])
            hs.append(b if r == 0 else a * hs[-1] + b)
            ps.append(a if r == 0 else a * ps[-1])
        a2, b2 = _group_scan(ps[-1], hs[-1])
        state0 = hst_ref[:, cs]
        carry = state0
        ends = []
        for k in range(g_rows // SUBLANES):
            ks = slice(k * SUBLANES, (k + 1) * SUBLANES)
            end_k = a2[ks, :] * carry + b2[ks, :]
            ends.append(end_k)
            carry = jnp.broadcast_to(end_k[SUBLANES - 1:SUBLANES, :], (SUBLANES, CHUNK))
        hst_ref[:, cs] = carry
        h_in = jnp.where(is_row0, state0[0:1, :], pltpu.roll(jnp.concatenate(ends, axis=0), 1, 0))
        for r in range(n_blocks):
            h_r = hs[r] + ps[r] * h_in
            for s in range(n_slabs):
                p_out[s, pl.ds(r, g_rows, stride=n_blocks), :] = h_r[:, s * LANES:(s + 1) * LANES]
        h_time = jnp.concatenate([p_out[s] for s in range(n_slabs)], axis=1)
        o_ref[0, :, cs] = (h_time * _gelu_tanh(ug)).astype(BF16)


def _sc_kernel(h_ref, wb_ref, wc_ref, wx_ref, cw_ref, o_ref, tail_ref):
    i = pl.program_id(1)
    t = h_ref.shape[1]
    n_chunks = o_ref.shape[2] // CHUNK
    k_conv = cw_ref.shape[0]

    h = h_ref[0]
    for j in range(n_chunks):
        cs = slice(j * CHUNK, (j + 1) * CHUNK)
        ub = jnp.dot(h, wb_ref[:, cs], preferred_element_type=F32)
        uc = jnp.dot(h, wc_ref[:, cs], preferred_element_type=F32)
        ux = jnp.dot(h, wx_ref[:, cs], preferred_element_type=F32)
        cx = uc * ux
        ext = jnp.concatenate([tail_ref[:, cs], cx], axis=0)
        tail_ref[:, cs] = cx[t - SUBLANES:, :]
        v = cw_ref[k_conv - 1:k_conv, cs] * cx
        for k in range(k_conv - 1):
            v = v + cw_ref[k:k + 1, cs] * _shift_rows(ext, k_conv - 1 - k)[SUBLANES:, :]
        o_ref[0, :, cs] = (ub * v).astype(BF16)


POOL_TAIL = 16


def _pool_kernel(h_ref, wp_ref, o_ref, tail_ref):
    i = pl.program_id(1)
    t = h_ref.shape[1]
    group_dim = o_ref.shape[2] // len(POOL_WINDOWS)

    h = h_ref[0]
    pos = lax.broadcasted_iota(jnp.int32, (t, group_dim), 0) + i * t
    for g, w in enumerate(POOL_WINDOWS):
        cs = slice(g * group_dim, (g + 1) * group_dim)
        u = jnp.dot(h, wp_ref[:, cs], preferred_element_type=F32)
        acc = jnp.concatenate([tail_ref[:, cs], u], axis=0)
        tail_ref[:, cs] = u[t - POOL_TAIL:, :]
        d = 1
        while d < w:
            acc = acc + pltpu.roll(acc, d, 0)
            d *= 2
        cnt = jnp.minimum(pos + 1, w).astype(F32)
        o_ref[0, :, cs] = (acc[POOL_TAIL:, :] / cnt - u).astype(BF16)


def _mix_kernel(h_ref, wlx_ref, wlg_ref, wsb_ref, wsc_ref, wsx_ref, wp_ref,
                lcw_ref, lcb_ref, wab_ref, ba_ref, bx_ref, lam_ref, scw_ref,
                ol_ref, os_ref, op_ref,
                ltail_ref, hst_ref, perm_ref, stail_ref, ptail_ref):
    @pl.when(pl.program_id(1) == 0)
    def _():
        for ref in (ltail_ref, hst_ref, stail_ref, ptail_ref):
            ref[...] = jnp.zeros_like(ref)

    _lru_kernel(h_ref, wlx_ref, wlg_ref, lcw_ref, lcb_ref, wab_ref, ba_ref, bx_ref, lam_ref,
                ol_ref, ltail_ref, hst_ref, perm_ref)
    _sc_kernel(h_ref, wsb_ref, wsc_ref, wsx_ref, scw_ref, os_ref, stail_ref)
    _pool_kernel(h_ref, wp_ref, op_ref, ptail_ref)


def _mix_branches(h, w_in, l, lru_conv_w, lru_conv_b, wab, b_a, b_x, lam, sc_conv_w):
    bsz, s, d = h.shape
    width = lru_conv_w.shape[1]
    t = ROW_TILE
    w_view = lambda c: pl.BlockSpec((None, d, width), lambda b, i: (l, 0, c),
                                    pipeline_mode=pl.Buffered(1))
    full = lambda shape: pl.BlockSpec(shape, lambda b, i: (0,) * len(shape))
    out = pl.BlockSpec((1, t, width), lambda b, i: (b, i, 0))
    vec = lambda a: a.reshape(1, width)
    return pl.pallas_call(
        _mix_kernel,
        grid=(bsz, s // t),
        in_specs=[pl.BlockSpec((1, t, d), lambda b, i: (b, i, 0))]
        + [w_view(c) for c in range(6)]
        + [full(lru_conv_w.shape), full((1, width)),
           pl.BlockSpec((None,) + wab.shape[1:], lambda b, i: (l, 0, 0, 0)),
           full((1, width)), full((1, width)), full((1, width)), full(sc_conv_w.shape)],
        out_specs=[out, out, out],
        out_shape=[jax.ShapeDtypeStruct((bsz, s, width), BF16)] * 3,
        scratch_shapes=[
            pltpu.VMEM((SUBLANES, width), F32),
            pltpu.VMEM((SUBLANES, width), F32),
            pltpu.VMEM((2 * width // CHUNK, CHUNK // LANES, t, LANES), F32),
            pltpu.VMEM((SUBLANES, width), F32),
            pltpu.VMEM((POOL_TAIL, width), F32),
        ],
        compiler_params=_params(("arbitrary", "arbitrary")),
        name="mix_branches",
    )(h, *([w_in] * 6), lru_conv_w, vec(lru_conv_b), wab, vec(b_a), vec(b_x), vec(lam),
      sc_conv_w)


def _merge_kernel(h_ref, ul_ref, us_ref, up_ref, wg0_ref, wg1_ref, wg2_ref,
                  wl_ref, ws_ref, wp_ref, ps_ref, o_ref):
    h = h_ref[...]
    dot = functools.partial(jnp.dot, preferred_element_type=F32)
    m = _sigmoid(dot(h, wg0_ref[...])) * dot(ul_ref[...], wl_ref[...])
    m = m + _sigmoid(dot(h, wg1_ref[...])) * dot(us_ref[...], ws_ref[...])
    m = m + _sigmoid(dot(h, wg2_ref[...])) * (dot(up_ref[...], wp_ref[0]) * ps_ref[...])
    o_ref[...] = m.astype(BF16)


def _merge(h, u_lru, u_sc, u_pool, w_in, l, gate_col0, w_lru_out, w_sc_out, pool_w, pool_scale):
    n, d = h.shape
    t, tn = MERGE_ROW_TILE, MERGE_COL_TILE
    _, n_groups, group_in, group_out = pool_w.shape
    assert group_out == tn
    g0 = gate_col0 // tn
    per_branch = d // tn
    rows = lambda width: pl.BlockSpec((t, width), lambda i, c: (i, 0))
    gate = lambda br: pl.BlockSpec((None, d, tn), lambda i, c: (l, 0, g0 + br * per_branch + c))
    return pl.pallas_call(
        _merge_kernel,
        grid=(n // t, d // tn),
        in_specs=[
            rows(d), rows(u_lru.shape[1]), rows(u_sc.shape[1]),
            pl.BlockSpec((t, group_in), lambda i, c: (i, c)),
            gate(0), gate(1), gate(2),
            pl.BlockSpec((None, w_lru_out.shape[1], tn), lambda i, c: (l, 0, c)),
            pl.BlockSpec((None, w_sc_out.shape[1], tn), lambda i, c: (l, 0, c)),
            pl.BlockSpec((None, 1, group_in, group_out), lambda i, c: (l, c, 0, 0)),
            pl.BlockSpec((1, tn), lambda i, c: (0, c)),
        ],
        out_specs=pl.BlockSpec((t, tn), lambda i, c: (i, c)),
        out_shape=jax.ShapeDtypeStruct((n, d), BF16),
        compiler_params=_params(("arbitrary", "arbitrary")),
        name="merge",
    )(h, u_lru, u_sc, u_pool, w_in, w_in, w_in, w_lru_out, w_sc_out, pool_w,
      pool_scale.reshape(1, d))


def _out_kernel(m_ref, w_ref, x_ref, g_ref, gain_ref, sc_ref, sh_ref, xo_ref, ho_ref):
    y = jnp.dot(m_ref[...], w_ref[...], preferred_element_type=F32)
    x_new = x_ref[...] + g_ref[0] * y
    xo_ref[...] = x_new
    ho_ref[...] = _mod_norm(x_new, gain_ref[...], sc_ref[0], sh_ref[0]).astype(BF16)


def _out_proj(merged, w_o, l, x, gate, gain, sc, sh, rows_per_batch):
    n, d = x.shape
    t = OUT_ROW_TILE
    tiles_per_batch = rows_per_batch // t
    rows = pl.BlockSpec((t, d), lambda i: (i, 0))
    vec = pl.BlockSpec((1, 1, d), lambda i: (i // tiles_per_batch, 0, 0))
    return pl.pallas_call(
        _out_kernel,
        grid=(n // t,),
        in_specs=[rows, pl.BlockSpec((None, d, d), lambda i: (l, 0, 0)), rows, vec,
                  pl.BlockSpec((1, d), lambda i: (0, 0)), vec, vec],
        out_specs=[rows, rows],
        out_shape=[jax.ShapeDtypeStruct((n, d), F32), jax.ShapeDtypeStruct((n, d), BF16)],
        compiler_params=_params(("arbitrary",)),
        name="out_proj",
    )(merged, w_o, x, gate, gain.reshape(1, d), sc, sh)


def _ffn_kernel(h_ref, w1_ref, w3_ref, w2_ref, x_ref, g_ref, gain_ref, *rest, final):
    if final:
        (xo_ref,) = rest
    else:
        sc_ref, sh_ref, xo_ref, ho_ref = rest
    acc_ref = xo_ref
    f = pl.program_id(1)

    @pl.when(f == 0)
    def _():
        acc_ref[...] = jnp.zeros_like(acc_ref)

    h = h_ref[...]
    a = jnp.dot(h, w1_ref[...], preferred_element_type=F32)
    b = jnp.dot(h, w3_ref[...], preferred_element_type=F32)
    act = ((a * _sigmoid(a)) * b).astype(BF16)
    acc_ref[...] += jnp.dot(act, w2_ref[...], preferred_element_type=F32)

    @pl.when(f == pl.num_programs(1) - 1)
    def _():
        x_new = x_ref[...] + g_ref[0] * acc_ref[...]
        if final:
            xo_ref[...] = x_new * lax.rsqrt(
                jnp.mean(x_new * x_new, axis=-1, keepdims=True) + EPS) * gain_ref[...]
        else:
            xo_ref[...] = x_new
            ho_ref[...] = _mod_norm(x_new, gain_ref[...], sc_ref[0], sh_ref[0]).astype(BF16)


def _ffn(h, w1, w3, w2, l, x, gate, gain, sc, sh, rows_per_batch, final):
    n, d = x.shape
    d_ff = w1.shape[2]
    t, tf = FFN_ROW_TILE, FFN_COL_TILE
    tiles_per_batch = rows_per_batch // t
    rows = pl.BlockSpec((t, d), lambda i, f: (i, 0))
    vec = pl.BlockSpec((1, 1, d), lambda i, f: (i // tiles_per_batch, 0, 0))
    in_specs = [rows,
                pl.BlockSpec((None, d, tf), lambda i, f: (l, 0, f)),
                pl.BlockSpec((None, d, tf), lambda i, f: (l, 0, f)),
                pl.BlockSpec((None, tf, d), lambda i, f: (l, f, 0)),
                rows, vec, pl.BlockSpec((1, d), lambda i, f: (0, 0))]
    args = [h, w1, w3, w2, x, gate, gain.reshape(1, d)]
    if final:
        out_specs = rows
        out_shape = jax.ShapeDtypeStruct((n, d), F32)
        scratch = []
    else:
        in_specs += [vec, vec]
        args += [sc, sh]
        out_specs = [rows, rows]
        out_shape = [jax.ShapeDtypeStruct((n, d), F32), jax.ShapeDtypeStruct((n, d), BF16)]
        scratch = []
    return pl.pallas_call(
        functools.partial(_ffn_kernel, final=final),
        grid=(n // t, d_ff // tf),
        in_specs=in_specs,
        out_specs=out_specs,
        out_shape=out_shape,
        scratch_shapes=scratch,
        compiler_params=_params(("arbitrary", "arbitrary")),
        name="ffn_final" if final else "ffn",
    )(*args)


def _block_diag_heads(w):
    depth, heads, hd, _ = w.shape
    per = CHUNK // hd
    w5 = w.reshape(depth, heads // per, per, hd, hd)
    eye = jnp.eye(per, dtype=w.dtype)
    bd = jnp.einsum("pq,ljpik->ljpiqk", eye, w5)
    return bd.reshape(depth, heads // per, CHUNK, CHUNK)


def kernel(x, c, w_mod, b_mod, norm_mix, w_in, lru_conv_w, lru_conv_b, lru_w_a, lru_b_a, lru_w_x, lru_b_x, lru_lambda, lru_w_out, sc_conv_w, sc_w_out, pool_w, pool_scale, w_o, norm_ffn, ffn_w1, ffn_w3, ffn_w2, norm_final):
    bsz, s, d = x.shape
    depth = w_in.shape[0]
    lru_width = lru_conv_w.shape[2]
    sc_width = sc_conv_w.shape[2]
    pool_width = pool_w.shape[1] * pool_w.shape[2]
    assert lru_width == sc_width == pool_width
    gate_col0 = 2 * lru_width + 3 * sc_width + pool_width

    mod = _modulation(c, w_mod, b_mod)
    w_in_b = w_in.astype(BF16)
    wab = jnp.concatenate([_block_diag_heads(lru_w_a), _block_diag_heads(lru_w_x)],
                          axis=-1).astype(BF16)
    lru_w_out_b = lru_w_out.astype(BF16)
    sc_w_out_b = sc_w_out.astype(BF16)
    pool_w_b = pool_w.astype(BF16)
    w_o_b = w_o.astype(BF16)
    w1_b, w3_b, w2_b = ffn_w1.astype(BF16), ffn_w3.astype(BF16), ffn_w2.astype(BF16)

    sh1, sc1, g1, sh2, sc2, g2 = (mod[0, k] for k in range(N_MOD))
    h = _first_norm(x, norm_mix[0], sc1, sh1)
    xf = x.reshape(bsz * s, d)
    for l in range(depth):
        sh1, sc1, g1, sh2, sc2, g2 = (mod[l, k] for k in range(N_MOD))
        u_lru, u_sc, u_pool = _mix_branches(h, w_in_b, l, lru_conv_w[l], lru_conv_b[l], wab,
                                            lru_b_a[l], lru_b_x[l], lru_lambda[l], sc_conv_w[l])
        flat = lambda a: a.reshape(bsz * s, a.shape[-1])
        merged = _merge(flat(h), flat(u_lru), flat(u_sc), flat(u_pool), w_in_b, l, gate_col0,
                        lru_w_out_b, sc_w_out_b, pool_w_b, pool_scale[l])
        xf, h2 = _out_proj(merged, w_o_b, l, xf, g1, norm_ffn[l], sc2, sh2, s)
        if l + 1 < depth:
            nsh1, nsc1 = mod[l + 1, 0], mod[l + 1, 1]
            xf, hn = _ffn(h2, w1_b, w3_b, w2_b, l, xf, g2, norm_mix[l + 1], nsc1, nsh1,
                          s, final=False)
            h = hn.reshape(bsz, s, d)
        else:
            xf = _ffn(h2, w1_b, w3_b, w2_b, l, xf, g2, norm_final, None, None, s,
                      final=True)
    return xf.reshape(bsz, s, d)
```
